```python
import jax
import jax.numpy as jnp
from jax import lax
import numpy as np

D_MODEL = 2048
BATCH = 4
SEQ = 2048
DEPTH = 2
DEC_BATCH = 8
DEC_SEQ = 4
PAST_LEN = 16384
PAGE_SIZE = 128

N_A_LAYERS = DEPTH // 2
N_B_LAYERS = DEPTH - N_A_LAYERS
CHUNK = 128
D_A = 2 * D_MODEL
N_GROUPS_A = 16
GROUP_DIM_A = D_A // N_GROUPS_A
N_HEADS = 16
HEAD_DIM = D_MODEL // N_HEADS
Q_BLOCK = 128
SB_LOGIT_INIT = -6.0
N_EXPERT_GROUPS = 8
EXPERTS_PER_GROUP = 8
N_EXPERTS = N_EXPERT_GROUPS * EXPERTS_PER_GROUP
TOP_K = 2
D_EXPERT = D_MODEL // 4
MOE_BLOCK = 128
DEEPNORM_ALPHA = (2 * DEPTH) ** 0.25
DEEPNORM_BETA = (8 * DEPTH) ** -0.25
LN_EPS = 1e-5

kernel_name = 'yoco_gmlp_stickbreak_hmoe_step'


def layer_norm(x, g, b):
    xf = x.astype(jnp.float32)
    mu = jnp.mean(xf, axis=-1, keepdims=True)
    var = jnp.mean(jnp.square(xf - mu), axis=-1, keepdims=True)
    y = (xf - mu) * lax.rsqrt(var + LN_EPS) * g.astype(jnp.float32) + b.astype(jnp.float32)
    return y.astype(x.dtype)


def chunk_gating_mlp(h, w_in, b_in, vn_g, vn_b, w_s, b_s, w_out):
    bsz, seq, _ = h.shape
    z = jax.nn.gelu(h @ w_in + b_in)
    u, v = jnp.split(z, 2, axis=-1)
    v = layer_norm(v, vn_g, vn_b)
    n_chunks = -(-seq // CHUNK)
    pad = n_chunks * CHUNK - seq
    vc = jnp.pad(v, ((0, 0), (0, pad), (0, 0))).reshape(bsz, n_chunks, CHUNK, N_GROUPS_A, GROUP_DIM_A)
    causal = jnp.tril(jnp.ones((CHUNK, CHUNK), dtype=bool))
    w_causal = jnp.where(causal, w_s, 0.0).astype(v.dtype)
    s = jnp.einsum('gts,bcsgd->bctgd', w_causal, vc) + b_s.T[None, None, :, :, None].astype(v.dtype)
    s = s.reshape(bsz, n_chunks * CHUNK, D_A)[:, :seq]
    return (u * s) @ w_out, v


def stick_breaking_attention(q, k_segs, v_segs, q_pos, k_pos, logit_bias):
    bsz, nq, nh, dh = q.shape
    blk = min(Q_BLOCK, nq)
    n_blk = -(-nq // blk)
    pad = n_blk * blk - nq
    q = jnp.pad(q, ((0, 0), (0, pad), (0, 0), (0, 0)))
    q_pos = jnp.pad(q_pos, (0, pad), constant_values=-1)
    q_blocks = q.reshape(bsz, n_blk, blk, nh, dh).transpose(1, 0, 3, 2, 4)
    pos_blocks = q_pos.reshape(n_blk, blk)
    k_t = [k.transpose(0, 2, 1, 3) for k in k_segs]
    v_t = [v.transpose(0, 2, 1, 3) for v in v_segs]
    seg_len = [k.shape[1] for k in k_segs]
    seg_off = [sum(seg_len[:i]) for i in range(len(seg_len))]
    scale = dh ** -0.5
    bias = logit_bias.astype(jnp.float32)[None, :, None, None]

    def one_block(args):
        qb, pb = args
        z = jnp.concatenate([jnp.einsum('bhqd,bhkd->bhqk', qb, k) for k in k_t], axis=-1).astype(jnp.float32) * scale + bias
        visible = k_pos[None, :] < pb[:, None]
        log_beta = jax.nn.log_sigmoid(z)
        log_stay = jnp.where(visible, jax.nn.log_sigmoid(-z), 0.0)
        later = lax.cumsum(log_stay, axis=3, reverse=True) - log_stay
        a = jnp.where(visible, jnp.exp(log_beta + later), 0.0)
        return sum(jnp.einsum('bhqk,bhkd->bhqd', a[..., o:o + n].astype(v.dtype), v)
                   for v, o, n in zip(v_t, seg_off, seg_len))

    out = lax.map(one_block, (q_blocks, pos_blocks))
    return out.transpose(1, 0, 3, 2, 4).reshape(bsz, n_blk * blk, nh, dh)[:, :nq]


def hierarchical_moe(h, rg_w, rg_b, re_w, re_b, w_gate, w_up, w_down):
    n_tok = h.shape[0]
    logits_g = (h @ rg_w + rg_b).astype(jnp.float32)
    probs_g = jax.nn.softmax(logits_g, axis=-1)
    g_sel = jnp.argmax(logits_g, axis=-1).astype(jnp.int32)
    g_weight = jnp.take_along_axis(probs_g, g_sel[:, None], axis=1)[:, 0]
    logits_e = (jnp.einsum('nd,gde->nge', h, re_w) + re_b).astype(jnp.float32)
    logits_e = jnp.take_along_axis(logits_e, g_sel[:, None, None], axis=1)[:, 0]
    top_v, top_i = lax.top_k(logits_e, TOP_K)
    gates = g_weight[:, None] * jax.nn.softmax(top_v, axis=-1)
    eid = (g_sel[:, None] * EXPERTS_PER_GROUP + top_i).reshape(-1).astype(jnp.int32)
    tok = jnp.repeat(jnp.arange(n_tok, dtype=jnp.int32), TOP_K)
    gate_flat = gates.reshape(-1)
    n_assign = n_tok * TOP_K
    blk = min(MOE_BLOCK, -(-n_assign // 8) * 8)
    order = jnp.argsort(eid, stable=True)
    e_sorted, tok_sorted, gate_sorted = eid[order], tok[order], gate_flat[order]
    counts = jnp.bincount(eid, length=N_EXPERTS).astype(jnp.int32)
    starts = jnp.cumsum(counts) - counts
    padded = (counts + blk - 1) // blk * blk
    pad_ends = jnp.cumsum(padded)
    pad_starts = pad_ends - padded
    dest = pad_starts[e_sorted] + jnp.arange(n_assign, dtype=jnp.int32) - starts[e_sorted]
    n_rows = -(-n_assign // blk) * blk + N_EXPERTS * blk
    n_blocks = n_rows // blk
    row_tok = jnp.zeros((n_rows,), jnp.int32).at[dest].set(tok_sorted)
    row_gate = jnp.zeros((n_rows,), jnp.float32).at[dest].set(gate_sorted)
    blk_expert = jnp.minimum(jnp.searchsorted(pad_ends, jnp.arange(n_blocks, dtype=jnp.int32) * blk, side='right'),
                             N_EXPERTS - 1)

    def expert_block(args):
        rows, e = args
        xb = h[rows]
        return (jax.nn.silu(xb @ w_gate[e]) * (xb @ w_up[e])) @ w_down[e]

    out = lax.map(expert_block, (row_tok.reshape(n_blocks, blk), blk_expert))
    out = out.reshape(n_rows, -1) * row_gate[:, None].astype(h.dtype)
    return jnp.zeros_like(h).at[row_tok].add(out)


def setup_inputs(seed: int = 0) -> dict:
    key = jax.random.key(seed)
    ks = list(jax.random.split(key, 40))

    def nrm(shape, std):
        return jax.random.normal(ks.pop(), shape, jnp.float32) * std

    hd = N_HEADS * HEAD_DIM
    n_pages = PAST_LEN // PAGE_SIZE
    n_used = DEC_BATCH * n_pages
    n_pool = n_used + n_used // 4
    page_table = jax.random.permutation(ks.pop(), n_pool)[:n_used].reshape(DEC_BATCH, n_pages).astype(jnp.int32)
    return {
        'x_prompt': nrm((BATCH, SEQ, D_MODEL), 1.0),
        'x_sample': nrm((DEC_BATCH, DEC_SEQ, D_MODEL), 1.0),
        'c_prompt': nrm((BATCH, D_MODEL), 1.0),
        'c_sample': nrm((DEC_BATCH, D_MODEL), 1.0),
        'cache_k': nrm((n_pool, PAGE_SIZE, N_HEADS, HEAD_DIM), 1.0),
        'cache_v': nrm((n_pool, PAGE_SIZE, N_HEADS, HEAD_DIM), 1.0),
        'page_table': page_table,
        'ada_w': nrm((DEPTH, D_MODEL, 6 * D_MODEL), 0.1 * D_MODEL ** -0.5),
        'ada_b': nrm((DEPTH, 6 * D_MODEL), 0.01),
        'ln1_g': 1.0 + nrm((DEPTH, D_MODEL), 0.01),
        'ln1_b': nrm((DEPTH, D_MODEL), 0.01),
        'ln2_g': 1.0 + nrm((DEPTH, D_MODEL), 0.01),
        'ln2_b': nrm((DEPTH, D_MODEL), 0.01),
        'a_w_in': nrm((N_A_LAYERS, D_MODEL, 2 * D_A), D_MODEL ** -0.5),
        'a_b_in': nrm((N_A_LAYERS, 2 * D_A), 0.01),
        'a_vnorm_g': 1.0 + nrm((N_A_LAYERS, D_A), 0.01),
        'a_vnorm_b': nrm((N_A_LAYERS, D_A), 0.01),
        'a_w_s': nrm((N_A_LAYERS, N_GROUPS_A, CHUNK, CHUNK), CHUNK ** -0.5),
        'a_b_s': 1.0 + nrm((N_A_LAYERS, N_GROUPS_A, CHUNK), 0.01),
        'a_w_out': nrm((N_A_LAYERS, D_A, D_MODEL), D_A ** -0.5 * DEEPNORM_BETA),
        'kv_w': jnp.concatenate([nrm((D_MODEL, hd), D_MODEL ** -0.5),
                                 nrm((D_MODEL, hd), D_MODEL ** -0.5 * DEEPNORM_BETA)], axis=1),
        'b_w_q': nrm((N_B_LAYERS, D_MODEL, hd), D_MODEL ** -0.5),
        'b_w_o': nrm((N_B_LAYERS, hd, D_MODEL), hd ** -0.5 * DEEPNORM_BETA),
        'b_logit_bias': SB_LOGIT_INIT + nrm((N_B_LAYERS, N_HEADS), 0.1),
        'moe_router_group_w': nrm((DEPTH, D_MODEL, N_EXPERT_GROUPS), D_MODEL ** -0.5),
        'moe_router_group_b': nrm((DEPTH, N_EXPERT_GROUPS), 0.01),
        'moe_router_expert_w': nrm((DEPTH, N_EXPERT_GROUPS, D_MODEL, EXPERTS_PER_GROUP), D_MODEL ** -0.5),
        'moe_router_expert_b': nrm((DEPTH, N_EXPERT_GROUPS, EXPERTS_PER_GROUP), 0.01),
        'moe_w_gate': nrm((DEPTH, N_EXPERTS, D_MODEL, D_EXPERT), D_MODEL ** -0.5),
        'moe_w_up': nrm((DEPTH, N_EXPERTS, D_MODEL, D_EXPERT), D_MODEL ** -0.5),
        'moe_w_down': nrm((DEPTH, N_EXPERTS, D_EXPERT, D_MODEL), D_EXPERT ** -0.5 * DEEPNORM_BETA),
    }


def reference(x_prompt, x_sample, c_prompt, c_sample, cache_k, cache_v, page_table,
              ada_w, ada_b, ln1_g, ln1_b, ln2_g, ln2_b,
              a_w_in, a_b_in, a_vnorm_g, a_vnorm_b, a_w_s, a_b_s, a_w_out,
              kv_w, b_w_q, b_w_o, b_logit_bias,
              moe_router_group_w, moe_router_group_b, moe_router_expert_w, moe_router_expert_b,
              moe_w_gate, moe_w_up, moe_w_down):
    hd = N_HEADS * HEAD_DIM

    def forward(x, c, past_k, past_v):
        bsz, seq, _ = x.shape
        past_len = 0 if past_k is None else past_k.shape[1]
        chunk_v_rows = []
        k_new = v_new = None
        for l in range(DEPTH):
            mod = c @ ada_w[l] + ada_b[l]
            sh1, sc1, g1, sh2, sc2, g2 = [m[:, None, :] for m in jnp.split(mod, 6, axis=-1)]
            h = x * (1 + sc1) + sh1
            if l < N_A_LAYERS:
                y, v_rows = chunk_gating_mlp(h, a_w_in[l], a_b_in[l], a_vnorm_g[l], a_vnorm_b[l],
                                             a_w_s[l], a_b_s[l], a_w_out[l])
                chunk_v_rows.append(v_rows)
            else:
                if k_new is None:
                    kv = (x @ kv_w).reshape(bsz, seq, 2, N_HEADS, HEAD_DIM)
                    k_new, v_new = kv[:, :, 0], kv[:, :, 1]
                    if past_k is None:
                        k_segs, v_segs = (k_new,), (v_new,)
                    else:
                        k_segs, v_segs = (past_k, k_new), (past_v, v_new)
                    q_pos = past_len + jnp.arange(seq, dtype=jnp.int32)
                    k_pos = jnp.arange(past_len + seq, dtype=jnp.int32)
                j = l - N_A_LAYERS
                q = (h @ b_w_q[j]).reshape(bsz, seq, N_HEADS, HEAD_DIM)
                o = stick_breaking_attention(q, k_segs, v_segs, q_pos, k_pos, b_logit_bias[j])
                y = o.reshape(bsz, seq, hd) @ b_w_o[j]
            x = layer_norm(DEEPNORM_ALPHA * x + (1 + g1) * y, ln1_g[l], ln1_b[l])
            h = x * (1 + sc2) + sh2
            y = hierarchical_moe(h.reshape(bsz * seq, D_MODEL), moe_router_group_w[l], moe_router_group_b[l],
                                 moe_router_expert_w[l], moe_router_expert_b[l],
                                 moe_w_gate[l], moe_w_up[l], moe_w_down[l]).reshape(bsz, seq, D_MODEL)
            x = layer_norm(DEEPNORM_ALPHA * x + (1 + g2) * y, ln2_g[l], ln2_b[l])
        return x, chunk_v_rows, k_new, v_new

    y_prompt, _, k_prompt, v_prompt = forward(x_prompt, c_prompt, None, None)

    dec_batch, n_pages = page_table.shape
    page_size = cache_k.shape[1]
    past_k = cache_k[page_table].reshape(dec_batch, n_pages * page_size, N_HEADS, HEAD_DIM)
    past_v = cache_v[page_table].reshape(dec_batch, n_pages * page_size, N_HEADS, HEAD_DIM)
    y_sample, chunk_v_list, k_sample, v_sample = forward(x_sample, c_sample, past_k, past_v)
    chunk_v_sample = jnp.stack(chunk_v_list)
    return (y_prompt, y_sample, k_prompt, v_prompt, k_sample, v_sample, chunk_v_sample)
```

```python
import functools
import math

import jax
import jax.numpy as jnp
from jax import lax
from jax.experimental import pallas as pl
from jax.experimental.pallas import tpu as pltpu

F32 = jnp.float32
BF16 = jnp.bfloat16
LN_EPS = 1e-5
V7X_VMEM_LIMIT_BYTES = 56 * 1024 * 1024
LANES = 128
SUBLANES = 8
TOP_K = 2
ROUTE_WIDTH = LANES
EPILOGUE_ROWS = 128
SAMPLE_ROWS = 2 * SUBLANES


def _cparams(semantics):
    return pltpu.CompilerParams(dimension_semantics=semantics,
                                vmem_limit_bytes=V7X_VMEM_LIMIT_BYTES)


def _layer_norm(x, g, b):
    mu = jnp.mean(x, axis=-1, keepdims=True)
    xc = x - mu
    var = jnp.mean(xc * xc, axis=-1, keepdims=True)
    return xc * lax.rsqrt(var + LN_EPS) * g + b


def _gelu_tanh(x):
    c = math.sqrt(2.0 / math.pi)
    return x * (0.5 * (1.0 + jnp.tanh(c * (x + 0.044715 * (x * x * x)))))


def _log_sigmoid(z):
    return jnp.minimum(z, 0.0) - jnp.log1p(jnp.exp(-jnp.abs(z)))


def _mod_spec(per_token, ts, tiles_per_seq, d):
    if per_token:
        return pl.BlockSpec((1, ts, d), lambda i, *_: (0, i, 0))
    return pl.BlockSpec((1, 1, d), lambda i, *_: (i // tiles_per_seq, 0, 0))


def _mod_kernel(c_ref, w_ref, b_ref, o_ref):
    c = c_ref[...].astype(BF16)
    w = w_ref[0].astype(BF16)
    o_ref[0] = jnp.dot(c, w, preferred_element_type=F32) + b_ref[0]


def _modulation(c_all, ada_w, ada_b):
    depth, d, d6 = ada_w.shape
    rows = c_all.shape[0]
    tn = 1024 if d6 % 1024 == 0 else d6
    return pl.pallas_call(
        _mod_kernel,
        grid=(depth, d6 // tn),
        in_specs=[pl.BlockSpec((rows, d), lambda l, n: (0, 0)),
                  pl.BlockSpec((1, d, tn), lambda l, n: (l, 0, n)),
                  pl.BlockSpec((1, 1, tn), lambda l, n: (l, 0, n))],
        out_specs=pl.BlockSpec((1, rows, tn), lambda l, n: (l, 0, n)),
        out_shape=jax.ShapeDtypeStruct((depth, rows, d6), F32),
        compiler_params=_cparams(("arbitrary", "arbitrary")),
        name="adaln_modulation",
    )(c_all, ada_w, ada_b.reshape(depth, 1, d6))


def _proj_kernel(*refs, n_out, modulate, has_bias, act):
    refs = list(refs)
    x_ref = refs.pop(0)
    sc_ref = refs.pop(0) if modulate else None
    sh_ref = refs.pop(0) if modulate else None
    w_refs = [refs.pop(0) for _ in range(n_out)]
    b_refs = [refs.pop(0) for _ in range(n_out)] if has_bias else [None] * n_out
    o_refs = [refs.pop(0) for _ in range(n_out)]
    h_ref = refs.pop(0)

    @pl.when(pl.program_id(1) == 0)
    def _():
        x = x_ref[...]
        if modulate:
            x = x * (1.0 + sc_ref[0]) + sh_ref[0]
        h_ref[...] = x.astype(BF16)

    h = h_ref[...]
    for w_ref, b_ref, o_ref in zip(w_refs, b_refs, o_refs):
        z = jnp.dot(h, w_ref[...], preferred_element_type=F32)
        if has_bias:
            z = z + b_ref[...]
        if act:
            z = _gelu_tanh(z)
        o_ref[...] = z.astype(o_ref.dtype)


def _projection(x, w, col_starts, width, out_dtypes, *, ts, tiles_per_seq, mod=None, bias=None,
                act=False, name="projection"):
    n, d = x.shape
    tn = min(512, width)
    n_out = len(col_starts)
    modulate = mod is not None
    in_specs = [pl.BlockSpec((ts, d), lambda i, j: (i, 0))]
    args = [x]
    if modulate:
        sc, sh, per_token = mod
        in_specs += [_mod_spec(per_token, ts, tiles_per_seq, d)] * 2
        args += [sc, sh]
    for c0 in col_starts:
        in_specs.append(pl.BlockSpec((d, tn), functools.partial(lambda i, j, o: (0, j + o), o=c0 // tn)))
        args.append(w)
    if bias is not None:
        for c0 in col_starts:
            in_specs.append(pl.BlockSpec((1, tn), functools.partial(lambda i, j, o: (0, j + o), o=c0 // tn)))
            args.append(bias)
    return pl.pallas_call(
        functools.partial(_proj_kernel, n_out=n_out, modulate=modulate, has_bias=bias is not None, act=act),
        grid=(n // ts, width // tn),
        in_specs=in_specs,
        out_specs=[pl.BlockSpec((ts, tn), lambda i, j: (i, j))] * n_out,
        out_shape=[jax.ShapeDtypeStruct((n, width), dt) for dt in out_dtypes],
        scratch_shapes=[pltpu.VMEM((ts, d), BF16)],
        compiler_params=_cparams(("parallel", "arbitrary")),
        name=name,
    )(*args)


def _gate_kernel(v_ref, u_ref, g_ref, b_ref, ws_ref, bs_ref, o_ref, *maybe_vn_ref, n_groups):
    v = v_ref[0]
    vn = _layer_norm(v, g_ref[...], b_ref[...])
    if maybe_vn_ref:
        maybe_vn_ref[0][0] = vn
    vb = vn.astype(BF16)
    rows, d_a = v.shape
    gd = d_a // n_groups
    causal = (lax.broadcasted_iota(jnp.int32, (rows, rows), 0)
              >= lax.broadcasted_iota(jnp.int32, (rows, rows), 1))
    for g in range(n_groups):
        w = jnp.where(causal, ws_ref[g], 0.0).astype(BF16)
        s = jnp.dot(w, vb[:, g * gd:(g + 1) * gd], preferred_element_type=F32) + bs_ref[:, g:g + 1]
        o_ref[0, :, g * gd:(g + 1) * gd] = (u_ref[0, :, g * gd:(g + 1) * gd].astype(F32) * s).astype(o_ref.dtype)


def _spatial_gate(v_raw, u, vn_g, vn_b, w_s, b_s_t, *, emit_vn):
    c, r, d_a = v_raw.shape
    n_groups = w_s.shape[0]
    blk = pl.BlockSpec((1, r, d_a), lambda i: (i, 0, 0))
    out_shape = [jax.ShapeDtypeStruct((c, r, d_a), BF16)]
    out_specs = [blk]
    if emit_vn:
        out_shape.append(jax.ShapeDtypeStruct((c, r, d_a), F32))
        out_specs.append(blk)
    return pl.pallas_call(
        functools.partial(_gate_kernel, n_groups=n_groups),
        grid=(c,),
        in_specs=[blk, blk,
                  pl.BlockSpec((1, d_a), lambda i: (0, 0)),
                  pl.BlockSpec((1, d_a), lambda i: (0, 0)),
                  pl.BlockSpec((n_groups, r, r), lambda i: (0, 0, 0)),
                  pl.BlockSpec((r, n_groups), lambda i: (0, 0))],
        out_specs=out_specs,
        out_shape=out_shape,
        compiler_params=_cparams(("parallel",)),
        name="spatial_gate",
    )(v_raw, u, vn_g, vn_b, w_s, b_s_t)


def _route(logits, n_groups, n_per):
    lane = lax.broadcasted_iota(jnp.int32, logits.shape, 1)
    lane_f = lane.astype(F32)
    neg = -jnp.inf
    big = float(2 * ROUTE_WIDTH)
    gl = jnp.where(lane < n_groups, logits, neg)
    gmax = jnp.max(gl, axis=-1, keepdims=True)
    g_sel = jnp.min(jnp.where(gl == gmax, lane_f, big), axis=-1, keepdims=True)
    g_weight = 1.0 / jnp.sum(jnp.exp(gl - gmax), axis=-1, keepdims=True)
    e_lane = lane_f - float(n_groups)
    in_group = (e_lane >= g_sel * n_per) & (e_lane < (g_sel + 1.0) * n_per)
    el = jnp.where(in_group, logits, neg)
    v1 = jnp.max(el, axis=-1, keepdims=True)
    i1 = jnp.min(jnp.where(el == v1, lane_f, big), axis=-1, keepdims=True)
    el2 = jnp.where(lane_f == i1, neg, el)
    v2 = jnp.max(el2, axis=-1, keepdims=True)
    i2 = jnp.min(jnp.where(el2 == v2, lane_f, big), axis=-1, keepdims=True)
    e2 = jnp.exp(v2 - v1)
    gate1 = g_weight * (1.0 / (1.0 + e2))
    gate2 = g_weight * (e2 / (1.0 + e2))
    id1 = i1 - float(n_groups)
    id2 = i2 - float(n_groups)
    return jnp.where(lane == 0, id1,
                     jnp.where(lane == 1, id2,
                               jnp.where(lane == 2, gate1,
                                         jnp.where(lane == 3, gate2, 0.0))))


def _outproj_kernel(a_ref, w_ref, x_ref, g1_ref, lng_ref, lnb_ref, sc2_ref, sh2_ref, wr_ref, br_ref,
                    x1_ref, h2_ref, rt_ref, acc_ref, *, alpha, n_groups, n_per):
    k = pl.program_id(1)

    @pl.when(k == 0)
    def _():
        acc_ref[...] = jnp.zeros_like(acc_ref)

    acc_ref[...] += jnp.dot(a_ref[...], w_ref[...], preferred_element_type=F32)

    @pl.when(k == pl.num_programs(1) - 1)
    def _():
        rows = min(EPILOGUE_ROWS, acc_ref.shape[0])

        def slab(s, c):
            r = pl.ds(pl.multiple_of(s * rows, rows), rows)
            g1 = g1_ref[0] if g1_ref.shape[1] == 1 else g1_ref[0, r, :]
            sc2 = sc2_ref[0] if sc2_ref.shape[1] == 1 else sc2_ref[0, r, :]
            sh2 = sh2_ref[0] if sh2_ref.shape[1] == 1 else sh2_ref[0, r, :]
            t = alpha * x_ref[r, :] + (1.0 + g1) * acc_ref[r, :]
            x1 = _layer_norm(t, lng_ref[...], lnb_ref[...])
            x1_ref[r, :] = x1
            h2 = x1 * (1.0 + sc2) + sh2
            h2_ref[r, :] = h2
            logits = jnp.dot(h2, wr_ref[...], preferred_element_type=F32,
                             precision=lax.Precision.HIGHEST) + br_ref[...]
            rt_ref[r, :] = _route(logits, n_groups, n_per)
            return c

        lax.fori_loop(0, acc_ref.shape[0] // rows, slab, 0)


def _out_projection(a, w, x, g1, ln_g, ln_b, sc2, sh2, w_route, b_route, *, per_token, ts, tiles_per_seq,
                    alpha, n_groups, n_per):
    n, kdim = a.shape
    d = w.shape[1]
    tk = min(512, kdim)
    mspec = _mod_spec(per_token, ts, tiles_per_seq, d)
    vec = pl.BlockSpec((1, d), lambda i, k: (0, 0))
    tile = pl.BlockSpec((ts, d), lambda i, k: (i, 0))
    return pl.pallas_call(
        functools.partial(_outproj_kernel, alpha=alpha, n_groups=n_groups, n_per=n_per),
        grid=(n // ts, kdim // tk),
        in_specs=[pl.BlockSpec((ts, tk), lambda i, k: (i, k)),
                  pl.BlockSpec((tk, d), lambda i, k: (k, 0)),
                  tile, mspec, vec, vec, mspec, mspec,
                  pl.BlockSpec((d, ROUTE_WIDTH), lambda i, k: (0, 0)),
                  pl.BlockSpec((1, ROUTE_WIDTH), lambda i, k: (0, 0))],
        out_specs=[tile, tile, pl.BlockSpec((ts, ROUTE_WIDTH), lambda i, k: (i, 0))],
        out_shape=[jax.ShapeDtypeStruct((n, d), F32), jax.ShapeDtypeStruct((n, d), F32),
                   jax.ShapeDtypeStruct((n, ROUTE_WIDTH), F32)],
        scratch_shapes=[pltpu.VMEM((ts, d), F32)],
        compiler_params=_cparams(("parallel", "arbitrary")),
        name="mixer_out_ln_route",
    )(a, w, x, g1, ln_g, ln_b, sc2, sh2, w_route, b_route)


def _moe_kernel(be_ref, nv_ref, rd_ref, h_hbm, wg_ref, wu_ref, wd_ref, gate_ref, out_hbm,
                xbuf, obuf, wgb, wub, wdb, gsem, ssem, *, tm):
    i = pl.program_id(0)
    nb = pl.num_programs(0)
    slot = i % 2

    def gather_copy(blk, s, r):
        tok = rd_ref[blk * tm + r] >> 1
        return pltpu.make_async_copy(h_hbm.at[pl.ds(tok, 1)], xbuf.at[s, pl.ds(r, 1)], gsem.at[s])

    def scatter_copy(blk, s, r):
        dst = rd_ref[blk * tm + r]
        return pltpu.make_async_copy(obuf.at[s, pl.ds(r, 1)], out_hbm.at[pl.ds(dst, 1)], ssem.at[s])

    def start_gather(blk, s):
        def body(r, c):
            gather_copy(blk, s, r).start()
            return c
        lax.fori_loop(0, tm, body, 0)

    def wait_scatter(blk, s):
        def body(r, c):
            scatter_copy(blk, s, r).wait()
            return c
        lax.fori_loop(0, nv_ref[blk], body, 0)

    @pl.when((i == 0) & (nv_ref[0] > 0))
    def _():
        start_gather(0, 0)

    nxt = jnp.minimum(i + 1, nb - 1)

    @pl.when((i + 1 < nb) & (nv_ref[nxt] > 0))
    def _():
        start_gather(nxt, 1 - slot)

    @pl.when(i >= 2)
    def _():
        wait_scatter(jnp.maximum(i - 2, 0), slot)

    @pl.when(nv_ref[i] > 0)
    def _():
        def wbody(r, c):
            gather_copy(i, slot, r).wait()
            return c
        lax.fori_loop(0, tm, wbody, 0)

        prev = jnp.maximum(i - 1, 0)

        @pl.when((i == 0) | (be_ref[i] != be_ref[prev]))
        def _():
            wgb[...] = wg_ref[0].astype(BF16)
            wub[...] = wu_ref[0].astype(BF16)
            wdb[...] = wd_ref[0].astype(BF16)

        x = xbuf[slot].astype(BF16)
        g = jnp.dot(x, wgb[...], preferred_element_type=F32)
        u = jnp.dot(x, wub[...], preferred_element_type=F32)
        mid = (g * jax.nn.sigmoid(g) * u).astype(BF16)
        o = jnp.dot(mid, wdb[...], preferred_element_type=F32)
        obuf[slot] = o * gate_ref[...]

        def sbody(r, c):
            scatter_copy(i, slot, r).start()
            return c
        lax.fori_loop(0, nv_ref[i], sbody, 0)

    @pl.when(i == nb - 1)
    def _():
        @pl.when(i >= 1)
        def _():
            wait_scatter(jnp.maximum(i - 1, 0), 1 - slot)
        wait_scatter(i, slot)


def _moe_experts(h2, blk_expert, blk_nvalid, row_dst, row_gate, w_gate, w_up, w_down, *, tm):
    n, d = h2.shape
    n_experts, _, de = w_gate.shape
    nb = blk_expert.shape[0]
    grid_spec = pltpu.PrefetchScalarGridSpec(
        num_scalar_prefetch=3,
        grid=(nb,),
        in_specs=[pl.BlockSpec(memory_space=pl.ANY),
                  pl.BlockSpec((1, d, de), lambda i, be, nv, rd: (be[i], 0, 0)),
                  pl.BlockSpec((1, d, de), lambda i, be, nv, rd: (be[i], 0, 0)),
                  pl.BlockSpec((1, de, d), lambda i, be, nv, rd: (be[i], 0, 0)),
                  pl.BlockSpec((tm, 1), lambda i, be, nv, rd: (i, 0))],
        out_specs=pl.BlockSpec(memory_space=pl.ANY),
        scratch_shapes=[pltpu.VMEM((2, tm, d), F32), pltpu.VMEM((2, tm, d), F32),
                        pltpu.VMEM((d, de), BF16), pltpu.VMEM((d, de), BF16), pltpu.VMEM((de, d), BF16),
                        pltpu.SemaphoreType.DMA((2,)), pltpu.SemaphoreType.DMA((2,))],
    )
    return pl.pallas_call(
        functools.partial(_moe_kernel, tm=tm),
        grid_spec=grid_spec,
        out_shape=jax.ShapeDtypeStruct((TOP_K * n, d), F32),
        compiler_params=_cparams(("arbitrary",)),
        name="moe_experts",
    )(blk_expert, blk_nvalid, row_dst, h2, w_gate, w_up, w_down, row_gate)


def _dispatch(route, n_experts, tm):
    n = route.shape[0]
    eid = route[:, :TOP_K].astype(jnp.int32).reshape(-1)
    gates = route[:, TOP_K:2 * TOP_K].reshape(-1)
    n_assign = n * TOP_K
    nb = -(-n_assign // tm) + n_experts
    order = jnp.argsort(eid, stable=True).astype(jnp.int32)
    e_sorted = eid[order]
    counts = jnp.bincount(eid, length=n_experts).astype(jnp.int32)
    starts = jnp.cumsum(counts) - counts
    nblk = (counts + tm - 1) // tm
    blk_end = jnp.cumsum(nblk)
    blk_start = blk_end - nblk
    dest = blk_start[e_sorted] * tm + jnp.arange(n_assign, dtype=jnp.int32) - starts[e_sorted]
    row_dst = jnp.zeros((nb * tm,), jnp.int32).at[dest].set(order)
    row_gate = jnp.zeros((nb * tm,), F32).at[dest].set(gates[order])
    blk = jnp.arange(nb, dtype=jnp.int32)
    n_used = blk_end[-1]
    be = jnp.minimum(jnp.searchsorted(blk_end, blk, side="right"), n_experts - 1).astype(jnp.int32)
    used = blk < n_used
    nv = jnp.where(used, jnp.clip(counts[be] - (blk - blk_start[be]) * tm, 0, tm), 0).astype(jnp.int32)
    last_e = be[jnp.maximum(n_used - 1, 0)]
    be = jnp.where(used, be, last_e)
    return be, nv, row_dst, row_gate.reshape(nb * tm, 1)


def _combine_kernel(x1_ref, o2_ref, g2_ref, lng_ref, lnb_ref, x2_ref, *, alpha):
    d = x1_ref.shape[-1]
    y = o2_ref[:, :d] + o2_ref[:, d:]
    t = alpha * x1_ref[...] + (1.0 + g2_ref[0]) * y
    x2_ref[...] = _layer_norm(t, lng_ref[...], lnb_ref[...])


def _combine(x1, out2, row_block_offset, g2, ln_g, ln_b, *, per_token, ts, tiles_per_seq, alpha):
    n, d = x1.shape
    vec = pl.BlockSpec((1, d), lambda i: (0, 0))
    tile = pl.BlockSpec((ts, d), lambda i: (i, 0))
    return pl.pallas_call(
        functools.partial(_combine_kernel, alpha=alpha),
        grid=(n // ts,),
        in_specs=[tile,
                  pl.BlockSpec((ts, TOP_K * d), lambda i: (i + row_block_offset, 0)),
                  _mod_spec(per_token, ts, tiles_per_seq, d), vec, vec],
        out_specs=tile,
        out_shape=jax.ShapeDtypeStruct((n, d), F32),
        compiler_params=_cparams(("parallel",)),
        name="moe_combine_ln",
    )(x1, out2, g2, ln_g, ln_b)


def _stick_block(z, vis, excl_ones, run):
    lb = _log_sigmoid(z)
    ls = lb - z
    if vis is not None:
        ls = jnp.where(vis, ls, 0.0)
    hi = ls.astype(BF16)
    lo = (ls - hi.astype(F32)).astype(BF16)
    later = (jnp.dot(hi, excl_ones, preferred_element_type=F32)
             + jnp.dot(lo, excl_ones, preferred_element_type=F32) + run)
    a = jnp.exp(lb + later)
    if vis is not None:
        a = jnp.where(vis, a, 0.0)
    return a, run + jnp.sum(ls, axis=-1, keepdims=True)


def _excl_ones(tk):
    j = lax.broadcasted_iota(jnp.int32, (tk, tk), 0)
    s = lax.broadcasted_iota(jnp.int32, (tk, tk), 1)
    return jnp.where(j > s, 1.0, 0.0).astype(BF16)


def _attn_kernel(bias_ref, q_ref, k_ref, v_ref, o_ref, *, tq, tk, scale):
    h = pl.program_id(1)
    qi = pl.program_id(2)
    bias = bias_ref[h]
    q = q_ref[0]
    dh = q.shape[-1]
    ones = _excl_ones(tk)
    q_pos = qi * tq + lax.broadcasted_iota(jnp.int32, (tq, tk), 0)
    k_off = lax.broadcasted_iota(jnp.int32, (tq, tk), 1)
    n_kb = (qi * tq + tq + tk - 1) // tk

    def body(j, carry):
        acc, run = carry
        kb = n_kb - 1 - j
        start = pl.multiple_of(kb * tk, tk)
        k = k_ref[0, pl.ds(start, tk), :].astype(BF16)
        v = v_ref[0, pl.ds(start, tk), :].astype(BF16)
        z = lax.dot_general(q, k, (((1,), (1,)), ((), ())), preferred_element_type=F32) * scale + bias
        vis = (k_off + kb * tk) < q_pos
        a, run = _stick_block(z, vis, ones, run)
        acc = acc + jnp.dot(a.astype(BF16), v, preferred_element_type=F32)
        return acc, run

    acc, _ = lax.fori_loop(0, n_kb, body, (jnp.zeros((tq, dh), F32), jnp.zeros((tq, 1), F32)))
    o_ref[0] = acc.astype(o_ref.dtype)


def _attention_prompt(q, k, v, logit_bias, *, n_heads):
    b, s, hd = q.shape
    dh = hd // n_heads
    tq = tk = min(256, s)
    grid_spec = pltpu.PrefetchScalarGridSpec(
        num_scalar_prefetch=0,
        grid=(b, n_heads, s // tq),
        in_specs=[pl.BlockSpec(memory_space=pltpu.SMEM),
                  pl.BlockSpec((1, tq, dh), lambda bi, h, qi: (bi, qi, h)),
                  pl.BlockSpec((1, s, dh), lambda bi, h, qi: (bi, 0, h)),
                  pl.BlockSpec((1, s, dh), lambda bi, h, qi: (bi, 0, h))],
        out_specs=pl.BlockSpec((1, tq, dh), lambda bi, h, qi: (bi, qi, h)),
    )
    return pl.pallas_call(
        functools.partial(_attn_kernel, tq=tq, tk=tk, scale=dh ** -0.5),
        grid_spec=grid_spec,
        out_shape=jax.ShapeDtypeStruct((b, s, hd), BF16),
        compiler_params=_cparams(("parallel", "parallel", "arbitrary")),
        name="stick_attention_prompt",
    )(logit_bias, q, k, v)


def _attn_paged_kernel(pt_ref, q_ref, bias_ref, kc_ref, vc_ref, kn_ref, vn_ref, o_ref,
                       z_ref, a_ref, acc_ref, run_ref, *, n_heads, page, qpad, scale):
    step = pl.program_id(1)

    @pl.when(step == 0)
    def _():
        acc_ref[...] = jnp.zeros_like(acc_ref)
        run_ref[...] = jnp.zeros_like(run_ref)

    def process(k_ref, v_ref, new_keys):
        for h in range(n_heads):
            kh = k_ref[0, pl.ds(h, page, stride=n_heads), :].astype(BF16)
            qh = q_ref[0, h * qpad:(h + 1) * qpad, :]
            z_ref[h * qpad:(h + 1) * qpad, :] = lax.dot_general(
                qh, kh, (((1,), (1,)), ((), ())), preferred_element_type=F32)
        z = z_ref[...] * scale + bias_ref[...]
        vis = None
        if new_keys:
            t = lax.broadcasted_iota(jnp.int32, z.shape, 0) % qpad
            vis = lax.broadcasted_iota(jnp.int32, z.shape, 1) < t
        a, run = _stick_block(z, vis, _excl_ones(page), run_ref[...])
        run_ref[...] = run
        a_ref[...] = a.astype(BF16)
        for h in range(n_heads):
            vh = v_ref[0, pl.ds(h, page, stride=n_heads), :].astype(BF16)
            acc_ref[h * qpad:(h + 1) * qpad, :] += jnp.dot(
                a_ref[h * qpad:(h + 1) * qpad, :], vh, preferred_element_type=F32)

    @pl.when(step == 0)
    def _():
        process(kn_ref, vn_ref, True)

    @pl.when(step > 0)
    def _():
        process(kc_ref, vc_ref, False)

    @pl.when(step == pl.num_programs(1) - 1)
    def _():
        o_ref[0] = acc_ref[...]


def _attention_paged(q_rows, bias_col, cache_k, cache_v, k_new, v_new, page_table, *, n_heads, qpad):
    b, rows, dh = q_rows.shape
    n_pages = page_table.shape[1]
    page = cache_k.shape[1] // n_heads

    def page_map(bi, s, pt):
        return (pt[bi, n_pages - jnp.maximum(s, 1)], 0, 0)

    grid_spec = pltpu.PrefetchScalarGridSpec(
        num_scalar_prefetch=1,
        grid=(b, n_pages + 1),
        in_specs=[pl.BlockSpec((1, rows, dh), lambda bi, s, pt: (bi, 0, 0)),
                  pl.BlockSpec((rows, 1), lambda bi, s, pt: (0, 0)),
                  pl.BlockSpec((1, page * n_heads, dh), page_map),
                  pl.BlockSpec((1, page * n_heads, dh), page_map),
                  pl.BlockSpec((1, page * n_heads, dh), lambda bi, s, pt: (bi, 0, 0)),
                  pl.BlockSpec((1, page * n_heads, dh), lambda bi, s, pt: (bi, 0, 0))],
        out_specs=pl.BlockSpec((1, rows, dh), lambda bi, s, pt: (bi, 0, 0)),
        scratch_shapes=[pltpu.VMEM((rows, page), F32), pltpu.VMEM((rows, page), BF16),
                        pltpu.VMEM((rows, dh), F32), pltpu.VMEM((rows, 1), F32)],
    )
    return pl.pallas_call(
        functools.partial(_attn_paged_kernel, n_heads=n_heads, page=page, qpad=qpad, scale=dh ** -0.5),
        grid_spec=grid_spec,
        out_shape=jax.ShapeDtypeStruct((b, rows, dh), F32),
        compiler_params=_cparams(("parallel", "arbitrary")),
        name="stick_attention_paged",
    )(page_table, q_rows, bias_col, cache_k, cache_v, k_new, v_new)


def kernel(x_prompt, x_sample, c_prompt, c_sample, cache_k, cache_v, page_table, ada_w, ada_b, ln1_g, ln1_b, ln2_g, ln2_b, a_w_in, a_b_in, a_vnorm_g, a_vnorm_b, a_w_s, a_b_s, a_w_out, kv_w, b_w_q, b_w_o, b_logit_bias, moe_router_group_w, moe_router_group_b, moe_router_expert_w, moe_router_expert_b, moe_w_gate, moe_w_up, moe_w_down):
    bp, sp, d = x_prompt.shape
    bs, ss, _ = x_sample.shape
    depth = ada_w.shape[0]
    n_a = a_w_in.shape[0]
    d_a = a_w_out.shape[1]
    chunk = a_w_s.shape[-1]
    n_pool, page, n_heads, dh = cache_k.shape
    hd = n_heads * dh
    n_groups, n_per = moe_router_expert_b.shape[1:]
    n_experts = n_groups * n_per
    alpha = (2 * depth) ** 0.25
    assert n_groups + n_experts <= ROUTE_WIDTH and ss <= SUBLANES and sp % chunk == 0
    np_tok, ns_tok = bp * sp, bs * ss
    ts_p = min(512, sp)
    tiles_p = sp // ts_p
    assert np_tok % ns_tok == 0

    c_rows = -(-(bp + bs) // SUBLANES) * SUBLANES
    c_all = jnp.concatenate([c_prompt, c_sample, jnp.zeros((c_rows - bp - bs, d), F32)], axis=0)
    mod = _modulation(c_all, ada_w, ada_b)

    def mods(l, group):
        m = mod[l].reshape(c_rows, 6, d)
        if group == 0:
            return [m[:bp, j].reshape(bp, 1, d) for j in range(6)]
        return [jnp.repeat(m[bp:bp + bs, j], ss, axis=0).reshape(1, ns_tok, d) for j in range(6)]

    groups = [dict(x=x_prompt.reshape(np_tok, d), ts=ts_p, tiles=tiles_p, per_token=False, n=np_tok),
              dict(x=x_sample.reshape(ns_tok, d), ts=ns_tok, tiles=1, per_token=True, n=ns_tok)]
    tm = 128
    chunk_v = []
    k_out, v_out = [None, None], [None, None]

    for l in range(depth):
        w_route = jnp.concatenate(
            [moe_router_group_w[l],
             jnp.transpose(moe_router_expert_w[l], (1, 0, 2)).reshape(d, n_experts),
             jnp.zeros((d, ROUTE_WIDTH - n_groups - n_experts), F32)], axis=1)
        b_route = jnp.concatenate(
            [moe_router_group_b[l], moe_router_expert_b[l].reshape(-1),
             jnp.zeros((ROUTE_WIDTH - n_groups - n_experts,), F32)]).reshape(1, ROUTE_WIDTH)
        x1s, h2s, routes, g2s = [], [], [], []
        for gi, grp in enumerate(groups):
            sh1, sc1, g1, sh2, sc2, g2 = mods(l, gi)
            x, ts, tiles, per_token = grp["x"], grp["ts"], grp["tiles"], grp["per_token"]
            if l < n_a:
                u, v_raw = _projection(x, a_w_in[l].astype(BF16), (0, d_a), d_a, (BF16, F32), ts=ts,
                                       tiles_per_seq=tiles, mod=(sc1, sh1, per_token),
                                       bias=a_b_in[l].reshape(1, 2 * d_a), act=True, name="gmlp_in")
                bs_t = jnp.transpose(a_b_s[l])
                if gi == 0:
                    n_chunks = np_tok // chunk
                    gated, = _spatial_gate(v_raw.reshape(n_chunks, chunk, d_a), u.reshape(n_chunks, chunk, d_a),
                                           a_vnorm_g[l].reshape(1, d_a), a_vnorm_b[l].reshape(1, d_a),
                                           a_w_s[l], bs_t, emit_vn=False)
                    a_mix = gated.reshape(np_tok, d_a)
                else:
                    pad = ((0, 0), (0, SAMPLE_ROWS - ss), (0, 0))
                    gated, vn = _spatial_gate(jnp.pad(v_raw.reshape(bs, ss, d_a), pad),
                                              jnp.pad(u.reshape(bs, ss, d_a), pad),
                                              a_vnorm_g[l].reshape(1, d_a), a_vnorm_b[l].reshape(1, d_a),
                                              a_w_s[l][:, :SAMPLE_ROWS, :SAMPLE_ROWS], bs_t[:SAMPLE_ROWS], emit_vn=True)
                    a_mix = gated[:, :ss].reshape(ns_tok, d_a)
                    chunk_v.append(vn[:, :ss])
                w_mix = a_w_out[l].astype(BF16)
            else:
                j = l - n_a
                if k_out[gi] is None:
                    k_out[gi], v_out[gi] = _projection(x, kv_w.astype(BF16), (0, hd), hd, (F32, F32), ts=ts,
                                                       tiles_per_seq=tiles, name="kv_proj")
                q, = _projection(x, b_w_q[j].astype(BF16), (0,), hd, (BF16,), ts=ts, tiles_per_seq=tiles,
                                 mod=(sc1, sh1, per_token), name="q_proj")
                if gi == 0:
                    o = _attention_prompt(q.reshape(bp, sp, hd), k_out[gi].reshape(bp, sp, hd),
                                          v_out[gi].reshape(bp, sp, hd), b_logit_bias[j], n_heads=n_heads)
                    a_mix = o.reshape(np_tok, hd)
                else:
                    qpad = SUBLANES
                    q_rows = jnp.pad(jnp.transpose(q.reshape(bs, ss, n_heads, dh), (0, 2, 1, 3)),
                                     ((0, 0), (0, 0), (0, qpad - ss), (0, 0))).reshape(bs, n_heads * qpad, dh)

                    def new_page(t):
                        t = jnp.pad(t.reshape(bs, ss, n_heads, dh), ((0, 0), (0, page - ss), (0, 0), (0, 0)))
                        return t.reshape(bs, page * n_heads, dh)

                    o = _attention_paged(q_rows, jnp.repeat(b_logit_bias[j], qpad).reshape(n_heads * qpad, 1),
                                         cache_k.reshape(n_pool, page * n_heads, dh),
                                         cache_v.reshape(n_pool, page * n_heads, dh),
                                         new_page(k_out[gi]), new_page(v_out[gi]), page_table,
                                         n_heads=n_heads, qpad=qpad)
                    o = jnp.transpose(o.reshape(bs, n_heads, qpad, dh)[:, :, :ss], (0, 2, 1, 3))
                    a_mix = o.reshape(ns_tok, hd).astype(BF16)
                w_mix = b_w_o[j].astype(BF16)
            x1, h2, route = _out_projection(a_mix, w_mix, x, g1, ln1_g[l].reshape(1, d), ln1_b[l].reshape(1, d),
                                            sc2, sh2, w_route, b_route, per_token=per_token, ts=ts,
                                            tiles_per_seq=tiles, alpha=alpha, n_groups=n_groups, n_per=n_per)
            x1s.append(x1)
            h2s.append(h2)
            routes.append(route)
            g2s.append(g2)

        h2_all = jnp.concatenate(h2s, axis=0)
        be, nv, row_dst, row_gate = _dispatch(jnp.concatenate(routes, axis=0), n_experts, tm)
        out2 = _moe_experts(h2_all, be, nv, row_dst, row_gate, moe_w_gate[l], moe_w_up[l], moe_w_down[l], tm=tm)
        out2 = out2.reshape(np_tok + ns_tok, TOP_K * d)
        offsets = [0, np_tok // ns_tok]
        for gi, grp in enumerate(groups):
            grp["x"] = _combine(x1s[gi], out2, offsets[gi], g2s[gi], ln2_g[l].reshape(1, d), ln2_b[l].reshape(1, d),
                                per_token=grp["per_token"], ts=grp["ts"], tiles_per_seq=grp["tiles"], alpha=alpha)

    y_prompt = groups[0]["x"].reshape(bp, sp, d)
    y_sample = groups[1]["x"].reshape(bs, ss, d)
    return (y_prompt, y_sample,
            k_out[0].reshape(bp, sp, n_heads, dh), v_out[0].reshape(bp, sp, n_heads, dh),
            k_out[1].reshape(bs, ss, n_heads, dh), v_out[1].reshape(bs, ss, n_heads, dh),
            jnp.stack(chunk_v))
```

```python
import functools
import math

import jax
import jax.numpy as jnp
from jax import lax
from jax.experimental import pallas as pl
from jax.experimental.pallas import tpu as pltpu

F32 = jnp.float32
BF16 = jnp.bfloat16
LN_EPS = 1e-5
V7X_VMEM_LIMIT_BYTES = 56 * 1024 * 1024
LANES = 128
SUBLANES = 8
TOP_K = 2
ROUTE_WIDTH = LANES
EPILOGUE_ROWS = 128
DMA_UNROLL = 8
SAMPLE_ROWS = 2 * SUBLANES


def _cparams(semantics):
    return pltpu.CompilerParams(dimension_semantics=semantics,
                                vmem_limit_bytes=V7X_VMEM_LIMIT_BYTES)


def _layer_norm(x, g, b):
    mu = jnp.mean(x, axis=-1, keepdims=True)
    xc = x - mu
    var = jnp.mean(xc * xc, axis=-1, keepdims=True)
    return xc * lax.rsqrt(var + LN_EPS) * g + b


def _gelu_tanh(x):
    c = math.sqrt(2.0 / math.pi)
    return x * (0.5 * (1.0 + jnp.tanh(c * (x + 0.044715 * (x * x * x)))))


def _log_sigmoid(z):
    return jnp.minimum(z, 0.0) - jnp.log(1.0 + jnp.exp(-jnp.abs(z)))


def _mod_spec(per_token, ts, tiles_per_seq, d):
    if per_token:
        return pl.BlockSpec((1, ts, d), lambda i, *_: (0, i, 0))
    return pl.BlockSpec((1, 1, d), lambda i, *_: (i // tiles_per_seq, 0, 0))


def _mod_kernel(c_ref, w_ref, b_ref, o_ref):
    c = c_ref[...].astype(BF16)
    w = w_ref[0].astype(BF16)
    o_ref[0] = jnp.dot(c, w, preferred_element_type=F32) + b_ref[0]


def _modulation(c_all, ada_w, ada_b):
    depth, d, d6 = ada_w.shape
    rows = c_all.shape[0]
    tn = 1024 if d6 % 1024 == 0 else d6
    return pl.pallas_call(
        _mod_kernel,
        grid=(depth, d6 // tn),
        in_specs=[pl.BlockSpec((rows, d), lambda l, n: (0, 0)),
                  pl.BlockSpec((1, d, tn), lambda l, n: (l, 0, n)),
                  pl.BlockSpec((1, 1, tn), lambda l, n: (l, 0, n))],
        out_specs=pl.BlockSpec((1, rows, tn), lambda l, n: (l, 0, n)),
        out_shape=jax.ShapeDtypeStruct((depth, rows, d6), F32),
        compiler_params=_cparams(("arbitrary", "arbitrary")),
        name="adaln_modulation",
    )(c_all, ada_w, ada_b.reshape(depth, 1, d6))


def _proj_kernel(*refs, n_out, modulate, has_bias, act, out_scale):
    refs = list(refs)
    x_ref = refs.pop(0)
    sc_ref = refs.pop(0) if modulate else None
    sh_ref = refs.pop(0) if modulate else None
    w_refs = [refs.pop(0) for _ in range(n_out)]
    b_refs = [refs.pop(0) for _ in range(n_out)] if has_bias else [None] * n_out
    o_refs = [refs.pop(0) for _ in range(n_out)]
    h_ref = refs.pop(0)

    @pl.when(pl.program_id(1) == 0)
    def _():
        x = x_ref[...]
        if modulate:
            x = x * (1.0 + sc_ref[0]) + sh_ref[0]
        h_ref[...] = x.astype(BF16)

    h = h_ref[...]
    for w_ref, b_ref, o_ref in zip(w_refs, b_refs, o_refs):
        z = jnp.dot(h, w_ref[...], preferred_element_type=F32)
        if has_bias:
            z = z + b_ref[...]
        if act:
            z = _gelu_tanh(z)
        if out_scale != 1.0:
            z = z * out_scale
        o_ref[...] = z.astype(o_ref.dtype)


def _projection(x, w, col_starts, width, out_dtypes, *, ts, tiles_per_seq, mod=None, bias=None,
                act=False, out_scale=1.0, name="projection"):
    n, d = x.shape
    tn = min(512, width)
    n_out = len(col_starts)
    modulate = mod is not None
    in_specs = [pl.BlockSpec((ts, d), lambda i, j: (i, 0))]
    args = [x]
    if modulate:
        sc, sh, per_token = mod
        in_specs += [_mod_spec(per_token, ts, tiles_per_seq, d)] * 2
        args += [sc, sh]
    for c0 in col_starts:
        in_specs.append(pl.BlockSpec((d, tn), functools.partial(lambda i, j, o: (0, j + o), o=c0 // tn)))
        args.append(w)
    if bias is not None:
        for c0 in col_starts:
            in_specs.append(pl.BlockSpec((1, tn), functools.partial(lambda i, j, o: (0, j + o), o=c0 // tn)))
            args.append(bias)
    return pl.pallas_call(
        functools.partial(_proj_kernel, n_out=n_out, modulate=modulate, has_bias=bias is not None, act=act,
                          out_scale=out_scale),
        grid=(n // ts, width // tn),
        in_specs=in_specs,
        out_specs=[pl.BlockSpec((ts, tn), lambda i, j: (i, j))] * n_out,
        out_shape=[jax.ShapeDtypeStruct((n, width), dt) for dt in out_dtypes],
        scratch_shapes=[pltpu.VMEM((ts, d), BF16)],
        compiler_params=_cparams(("parallel", "arbitrary")),
        name=name,
    )(*args)


def _gate_kernel(v_ref, u_ref, g_ref, b_ref, ws_ref, bs_ref, o_ref, *maybe_vn_ref, n_groups):
    v = v_ref[0]
    vn = _layer_norm(v, g_ref[...], b_ref[...])
    if maybe_vn_ref:
        maybe_vn_ref[0][0] = vn
    vb = vn.astype(BF16)
    rows, d_a = v.shape
    gd = d_a // n_groups
    causal = (lax.broadcasted_iota(jnp.int32, (rows, rows), 0)
              >= lax.broadcasted_iota(jnp.int32, (rows, rows), 1))
    for g in range(n_groups):
        w = jnp.where(causal, ws_ref[g], 0.0).astype(BF16)
        s = jnp.dot(w, vb[:, g * gd:(g + 1) * gd], preferred_element_type=F32) + bs_ref[:, g:g + 1]
        o_ref[0, :, g * gd:(g + 1) * gd] = (u_ref[0, :, g * gd:(g + 1) * gd].astype(F32) * s).astype(o_ref.dtype)


def _spatial_gate(v_raw, u, vn_g, vn_b, w_s, b_s_t, *, emit_vn):
    c, r, d_a = v_raw.shape
    n_groups = w_s.shape[0]
    blk = pl.BlockSpec((1, r, d_a), lambda i: (i, 0, 0))
    out_shape = [jax.ShapeDtypeStruct((c, r, d_a), BF16)]
    out_specs = [blk]
    if emit_vn:
        out_shape.append(jax.ShapeDtypeStruct((c, r, d_a), F32))
        out_specs.append(blk)
    return pl.pallas_call(
        functools.partial(_gate_kernel, n_groups=n_groups),
        grid=(c,),
        in_specs=[blk, blk,
                  pl.BlockSpec((1, d_a), lambda i: (0, 0)),
                  pl.BlockSpec((1, d_a), lambda i: (0, 0)),
                  pl.BlockSpec((n_groups, r, r), lambda i: (0, 0, 0)),
                  pl.BlockSpec((r, n_groups), lambda i: (0, 0))],
        out_specs=out_specs,
        out_shape=out_shape,
        compiler_params=_cparams(("parallel",)),
        name="spatial_gate",
    )(v_raw, u, vn_g, vn_b, w_s, b_s_t)


def _route(logits, n_groups, n_per):
    lane = lax.broadcasted_iota(jnp.int32, logits.shape, 1)
    lane_f = lane.astype(F32)
    neg = -jnp.inf
    big = float(2 * ROUTE_WIDTH)
    gl = jnp.where(lane < n_groups, logits, neg)
    gmax = jnp.max(gl, axis=-1, keepdims=True)
    g_sel = jnp.min(jnp.where(gl == gmax, lane_f, big), axis=-1, keepdims=True)
    g_weight = 1.0 / jnp.sum(jnp.exp(gl - gmax), axis=-1, keepdims=True)
    e_lane = lane_f - float(n_groups)
    in_group = (e_lane >= g_sel * n_per) & (e_lane < (g_sel + 1.0) * n_per)
    el = jnp.where(in_group, logits, neg)
    v1 = jnp.max(el, axis=-1, keepdims=True)
    i1 = jnp.min(jnp.where(el == v1, lane_f, big), axis=-1, keepdims=True)
    el2 = jnp.where(lane_f == i1, neg, el)
    v2 = jnp.max(el2, axis=-1, keepdims=True)
    i2 = jnp.min(jnp.where(el2 == v2, lane_f, big), axis=-1, keepdims=True)
    e2 = jnp.exp(v2 - v1)
    gate1 = g_weight * (1.0 / (1.0 + e2))
    gate2 = g_weight * (e2 / (1.0 + e2))
    id1 = i1 - float(n_groups)
    id2 = i2 - float(n_groups)
    return jnp.where(lane == 0, id1,
                     jnp.where(lane == 1, id2,
                               jnp.where(lane == 2, gate1,
                                         jnp.where(lane == 3, gate2, 0.0))))


def _outproj_kernel(a_ref, w_ref, x_ref, g1_ref, lng_ref, lnb_ref, sc2_ref, sh2_ref, wr_ref, br_ref,
                    x1_ref, h2_ref, rt_ref, acc_ref, *, alpha, n_groups, n_per):
    k = pl.program_id(1)

    @pl.when(k == 0)
    def _():
        acc_ref[...] = jnp.zeros_like(acc_ref)

    acc_ref[...] += jnp.dot(a_ref[...], w_ref[...], preferred_element_type=F32)

    @pl.when(k == pl.num_programs(1) - 1)
    def _():
        rows = min(EPILOGUE_ROWS, acc_ref.shape[0])

        def slab(s, c):
            r = pl.ds(pl.multiple_of(s * rows, rows), rows)
            g1 = g1_ref[0] if g1_ref.shape[1] == 1 else g1_ref[0, r, :]
            sc2 = sc2_ref[0] if sc2_ref.shape[1] == 1 else sc2_ref[0, r, :]
            sh2 = sh2_ref[0] if sh2_ref.shape[1] == 1 else sh2_ref[0, r, :]
            t = alpha * x_ref[r, :] + (1.0 + g1) * acc_ref[r, :]
            x1 = _layer_norm(t, lng_ref[...], lnb_ref[...])
            x1_ref[r, :] = x1
            h2 = x1 * (1.0 + sc2) + sh2
            h2_ref[r, :] = h2
            logits = jnp.dot(h2.astype(BF16), wr_ref[...], preferred_element_type=F32) + br_ref[...]
            rt_ref[r, :] = _route(logits, n_groups, n_per)
            return c

        lax.fori_loop(0, acc_ref.shape[0] // rows, slab, 0)


def _out_projection(a, w, x, g1, ln_g, ln_b, sc2, sh2, w_route, b_route, *, per_token, ts, tiles_per_seq,
                    alpha, n_groups, n_per):
    n, kdim = a.shape
    d = w.shape[1]
    tk = min(512, kdim)
    mspec = _mod_spec(per_token, ts, tiles_per_seq, d)
    vec = pl.BlockSpec((1, d), lambda i, k: (0, 0))
    tile = pl.BlockSpec((ts, d), lambda i, k: (i, 0))
    return pl.pallas_call(
        functools.partial(_outproj_kernel, alpha=alpha, n_groups=n_groups, n_per=n_per),
        grid=(n // ts, kdim // tk),
        in_specs=[pl.BlockSpec((ts, tk), lambda i, k: (i, k)),
                  pl.BlockSpec((tk, d), lambda i, k: (k, 0)),
                  tile, mspec, vec, vec, mspec, mspec,
                  pl.BlockSpec((d, ROUTE_WIDTH), lambda i, k: (0, 0)),
                  pl.BlockSpec((1, ROUTE_WIDTH), lambda i, k: (0, 0))],
        out_specs=[tile, tile, pl.BlockSpec((ts, ROUTE_WIDTH), lambda i, k: (i, 0))],
        out_shape=[jax.ShapeDtypeStruct((n, d), F32), jax.ShapeDtypeStruct((n, d), F32),
                   jax.ShapeDtypeStruct((n, ROUTE_WIDTH), F32)],
        scratch_shapes=[pltpu.VMEM((ts, d), F32)],
        compiler_params=_cparams(("parallel", "arbitrary")),
        name="mixer_out_ln_route",
    )(a, w, x, g1, ln_g, ln_b, sc2, sh2, w_route, b_route)


def _moe_kernel(be_ref, nv_ref, rd_ref, h_hbm, wg_ref, wu_ref, wd_ref, gate_ref, out_hbm,
                xbuf, obuf, wgb, wub, wdb, gsem, ssem, *, tm):
    i = pl.program_id(0)
    nb = pl.num_programs(0)
    slot = i % 2

    def gather_copy(blk, s, r):
        tok = rd_ref[blk * tm + r] >> 1
        return pltpu.make_async_copy(h_hbm.at[pl.ds(tok, 1)], xbuf.at[s, pl.ds(r, 1)], gsem.at[s])

    def scatter_copy(blk, s, r):
        dst = rd_ref[blk * tm + r]
        return pltpu.make_async_copy(obuf.at[s, pl.ds(r, 1)], out_hbm.at[pl.ds(dst, 1)], ssem.at[s])

    def for_rows(n, fn):
        n_groups = n // DMA_UNROLL

        def group(gidx, c):
            for u in range(DMA_UNROLL):
                fn(gidx * DMA_UNROLL + u)
            return c

        def tail(r, c):
            fn(r)
            return c

        lax.fori_loop(0, n_groups, group, 0)
        lax.fori_loop(n_groups * DMA_UNROLL, n, tail, 0)

    def start_gather(blk, s):
        for_rows(nv_ref[blk], lambda r: gather_copy(blk, s, r).start())

    def wait_rows(n, s, gather):
        def copy(rows):
            if gather:
                return pltpu.make_async_copy(h_hbm.at[rows], xbuf.at[s, rows], gsem.at[s])
            return pltpu.make_async_copy(obuf.at[s, rows], out_hbm.at[rows], ssem.at[s])

        n_aligned = pl.multiple_of((n // SUBLANES) * SUBLANES, SUBLANES)

        @pl.when(n_aligned > 0)
        def _():
            copy(pl.ds(0, n_aligned)).wait()

        def tail(r, c):
            copy(pl.ds(0, 1)).wait()
            return c

        lax.fori_loop(n_aligned, n, tail, 0)

    @pl.when(i == 0)
    def _():
        xbuf[...] = jnp.zeros_like(xbuf)
        start_gather(0, 0)

    nxt = jnp.minimum(i + 1, nb - 1)

    @pl.when(i + 1 < nb)
    def _():
        start_gather(nxt, 1 - slot)

    @pl.when(i >= 2)
    def _():
        wait_rows(nv_ref[jnp.maximum(i - 2, 0)], slot, False)

    @pl.when(nv_ref[i] > 0)
    def _():
        wait_rows(nv_ref[i], slot, True)
        prev = jnp.maximum(i - 1, 0)

        @pl.when((i == 0) | (be_ref[i] != be_ref[prev]))
        def _():
            wgb[...] = wg_ref[0, 0].astype(BF16)
            wub[...] = wu_ref[0, 0].astype(BF16)
            wdb[...] = wd_ref[0, 0].astype(BF16)

        x = xbuf[slot].astype(BF16)
        g = jnp.dot(x, wgb[...], preferred_element_type=F32)
        u = jnp.dot(x, wub[...], preferred_element_type=F32)
        mid = (g * jax.nn.sigmoid(g) * u).astype(BF16)
        o = jnp.dot(mid, wdb[...], preferred_element_type=F32)
        obuf[slot] = o * gate_ref[...]
        for_rows(nv_ref[i], lambda r: scatter_copy(i, slot, r).start())

    @pl.when(i == nb - 1)
    def _():
        @pl.when(i >= 1)
        def _():
            wait_rows(nv_ref[jnp.maximum(i - 1, 0)], 1 - slot, False)
        wait_rows(nv_ref[i], slot, False)


def _moe_experts(h2, blk_expert, blk_nvalid, row_dst, row_gate, w_gate, w_up, w_down, layer, *, tm):
    n, d = h2.shape
    _, n_experts, _, de = w_gate.shape
    nb = blk_expert.shape[0]
    grid_spec = pltpu.PrefetchScalarGridSpec(
        num_scalar_prefetch=3,
        grid=(nb,),
        in_specs=[pl.BlockSpec(memory_space=pl.ANY),
                  pl.BlockSpec((1, 1, d, de), lambda i, be, nv, rd: (layer, be[i], 0, 0)),
                  pl.BlockSpec((1, 1, d, de), lambda i, be, nv, rd: (layer, be[i], 0, 0)),
                  pl.BlockSpec((1, 1, de, d), lambda i, be, nv, rd: (layer, be[i], 0, 0)),
                  pl.BlockSpec((tm, 1), lambda i, be, nv, rd: (i, 0))],
        out_specs=pl.BlockSpec(memory_space=pl.ANY),
        scratch_shapes=[pltpu.VMEM((2, tm, d), F32), pltpu.VMEM((2, tm, d), F32),
                        pltpu.VMEM((d, de), BF16), pltpu.VMEM((d, de), BF16), pltpu.VMEM((de, d), BF16),
                        pltpu.SemaphoreType.DMA((2,)), pltpu.SemaphoreType.DMA((2,))],
    )
    return pl.pallas_call(
        functools.partial(_moe_kernel, tm=tm),
        grid_spec=grid_spec,
        out_shape=jax.ShapeDtypeStruct((TOP_K * n, d), F32),
        compiler_params=_cparams(("arbitrary",)),
        name="moe_experts",
    )(blk_expert, blk_nvalid, row_dst, h2, w_gate, w_up, w_down, row_gate)


def _dispatch(route, n_experts, tm):
    n = route.shape[0]
    eid = route[:, :TOP_K].astype(jnp.int32).reshape(-1)
    gates = route[:, TOP_K:2 * TOP_K].reshape(-1)
    n_assign = n * TOP_K
    nb = -(-n_assign // tm) + n_experts
    order = jnp.argsort(eid, stable=True).astype(jnp.int32)
    e_sorted = eid[order]
    experts = jnp.arange(n_experts, dtype=jnp.int32)
    starts = jnp.searchsorted(e_sorted, experts, side="left").astype(jnp.int32)
    counts = jnp.searchsorted(e_sorted, experts, side="right").astype(jnp.int32) - starts
    nblk = (counts + tm - 1) // tm
    blk_end = jnp.cumsum(nblk)
    blk_start = blk_end - nblk
    blk = jnp.arange(nb, dtype=jnp.int32)
    n_used = blk_end[-1]
    be = jnp.minimum(jnp.searchsorted(blk_end, blk, side="right"), n_experts - 1).astype(jnp.int32)
    used = blk < n_used
    first_row = (blk - blk_start[be]) * tm
    nv = jnp.where(used, jnp.clip(counts[be] - first_row, 0, tm), 0).astype(jnp.int32)
    r = jnp.arange(tm, dtype=jnp.int32)[None, :]
    src = jnp.clip((starts[be] + first_row)[:, None] + r, 0, n_assign - 1)
    valid = r < nv[:, None]
    picked = order[src]
    row_dst = jnp.where(valid, picked, 0).reshape(nb * tm)
    row_gate = jnp.where(valid, gates[picked], 0.0).reshape(nb * tm, 1)
    last_e = be[jnp.maximum(n_used - 1, 0)]
    be = jnp.where(used, be, last_e)
    return be, nv, row_dst, row_gate


def _combine_kernel(x1_ref, o2_ref, g2_ref, lng_ref, lnb_ref, x2_ref, *, alpha):
    d = x1_ref.shape[-1]
    y = o2_ref[:, :d] + o2_ref[:, d:]
    t = alpha * x1_ref[...] + (1.0 + g2_ref[0]) * y
    x2_ref[...] = _layer_norm(t, lng_ref[...], lnb_ref[...])


def _combine(x1, out2, row_block_offset, g2, ln_g, ln_b, *, per_token, ts, tiles_per_seq, alpha):
    n, d = x1.shape
    vec = pl.BlockSpec((1, d), lambda i: (0, 0))
    tile = pl.BlockSpec((ts, d), lambda i: (i, 0))
    return pl.pallas_call(
        functools.partial(_combine_kernel, alpha=alpha),
        grid=(n // ts,),
        in_specs=[tile,
                  pl.BlockSpec((ts, TOP_K * d), lambda i: (i + row_block_offset, 0)),
                  _mod_spec(per_token, ts, tiles_per_seq, d), vec, vec],
        out_specs=tile,
        out_shape=jax.ShapeDtypeStruct((n, d), F32),
        compiler_params=_cparams(("parallel",)),
        name="moe_combine_ln",
    )(x1, out2, g2, ln_g, ln_b)


def _stick_block(z, vis, excl_ones, run):
    lb = _log_sigmoid(z)
    ls = lb - z
    if vis is not None:
        ls = jnp.where(vis, ls, 0.0)
    hi = ls.astype(BF16)
    lo = (ls - hi.astype(F32)).astype(BF16)
    later = (jnp.dot(hi, excl_ones, preferred_element_type=F32)
             + jnp.dot(lo, excl_ones, preferred_element_type=F32) + run)
    a = jnp.exp(lb + later)
    if vis is not None:
        a = jnp.where(vis, a, 0.0)
    return a, run + jnp.sum(ls, axis=-1, keepdims=True)


def _excl_ones(tk):
    j = lax.broadcasted_iota(jnp.int32, (tk, tk), 0)
    s = lax.broadcasted_iota(jnp.int32, (tk, tk), 1)
    return jnp.where(j > s, 1.0, 0.0).astype(BF16)


def _attn_kernel(bias_ref, q_ref, k_ref, v_ref, o_ref, *, t, dh, heads):
    scale = dh ** -0.5
    hg = pl.program_id(1)
    qi = pl.program_id(2)
    ones = _excl_ones(t)
    below = (lax.broadcasted_iota(jnp.int32, (t, t), 1) < lax.broadcasted_iota(jnp.int32, (t, t), 0))

    def block(start, carry, vis):
        out = []
        for hh in range(heads):
            acc, run = carry[hh]
            lanes = slice(hh * dh, (hh + 1) * dh)
            k = k_ref[0, pl.ds(start, t), lanes].astype(BF16)
            v = v_ref[0, pl.ds(start, t), lanes].astype(BF16)
            z = lax.dot_general(q_ref[0, :, lanes], k, (((1,), (1,)), ((), ())),
                                preferred_element_type=F32) * scale + bias_ref[hg * heads + hh]
            a, run = _stick_block(z, vis, ones, run)
            out.append((acc + jnp.dot(a.astype(BF16), v, preferred_element_type=F32), run))
        return tuple(out)

    init = tuple((jnp.zeros((t, dh), F32), jnp.zeros((t, 1), F32)) for _ in range(heads))
    carry = block(pl.multiple_of(qi * t, t), init, below)
    carry = lax.fori_loop(0, qi, lambda j, c: block(pl.multiple_of((qi - 1 - j) * t, t), c, None), carry)
    for hh in range(heads):
        o_ref[0, :, hh * dh:(hh + 1) * dh] = carry[hh][0].astype(o_ref.dtype)


def _attention_prompt(q, k, v, logit_bias, *, n_heads):
    b, s, hd = q.shape
    dh = hd // n_heads
    t = min(256, s)
    heads = 2 if n_heads % 2 == 0 else 1
    w = heads * dh
    return pl.pallas_call(
        functools.partial(_attn_kernel, t=t, dh=dh, heads=heads),
        grid=(b, n_heads // heads, s // t),
        in_specs=[pl.BlockSpec(memory_space=pltpu.SMEM),
                  pl.BlockSpec((1, t, w), lambda bi, h, qi: (bi, qi, h)),
                  pl.BlockSpec((1, s, w), lambda bi, h, qi: (bi, 0, h)),
                  pl.BlockSpec((1, s, w), lambda bi, h, qi: (bi, 0, h))],
        out_specs=pl.BlockSpec((1, t, w), lambda bi, h, qi: (bi, qi, h)),
        out_shape=jax.ShapeDtypeStruct((b, s, hd), BF16),
        compiler_params=_cparams(("parallel", "parallel", "arbitrary")),
        name="stick_attention_prompt",
    )(logit_bias, q, k, v)


def _attn_paged_kernel(pt_ref, q_ref, bias_ref, kc_ref, vc_ref, kn_ref, vn_ref, o_ref,
                       z_ref, a_ref, acc_ref, run_ref, *, n_heads, page, qpad):
    step = pl.program_id(1)

    @pl.when(step == 0)
    def _():
        acc_ref[...] = jnp.zeros_like(acc_ref)
        run_ref[...] = jnp.zeros_like(run_ref)

    def process(k_ref, v_ref, new_keys):
        for h in range(n_heads):
            kh = k_ref[0, pl.ds(h, page, stride=n_heads), :].astype(BF16)
            qh = q_ref[0, h * qpad:(h + 1) * qpad, :]
            z_ref[h * qpad:(h + 1) * qpad, :] = lax.dot_general(
                qh, kh, (((1,), (1,)), ((), ())), preferred_element_type=F32)
        z = z_ref[...] * (q_ref.shape[-1] ** -0.5) + bias_ref[...]
        vis = None
        if new_keys:
            t = lax.broadcasted_iota(jnp.int32, z.shape, 0) % qpad
            vis = lax.broadcasted_iota(jnp.int32, z.shape, 1) < t
        a, run = _stick_block(z, vis, _excl_ones(page), run_ref[...])
        run_ref[...] = run
        a_ref[...] = a.astype(BF16)
        for h in range(n_heads):
            vh = v_ref[0, pl.ds(h, page, stride=n_heads), :].astype(BF16)
            acc_ref[h * qpad:(h + 1) * qpad, :] += jnp.dot(
                a_ref[h * qpad:(h + 1) * qpad, :], vh, preferred_element_type=F32)

    @pl.when(step == 0)
    def _():
        process(kn_ref, vn_ref, True)

    @pl.when(step > 0)
    def _():
        process(kc_ref, vc_ref, False)

    @pl.when(step == pl.num_programs(1) - 1)
    def _():
        o_ref[0] = acc_ref[...]


def _attention_paged(q_rows, bias_col, cache_k, cache_v, k_new, v_new, page_table, *, n_heads, qpad):
    b, rows, dh = q_rows.shape
    n_pages = page_table.shape[1]
    page = cache_k.shape[1] // n_heads

    def page_map(bi, s, pt):
        return (pt[bi, n_pages - jnp.maximum(s, 1)], 0, 0)

    grid_spec = pltpu.PrefetchScalarGridSpec(
        num_scalar_prefetch=1,
        grid=(b, n_pages + 1),
        in_specs=[pl.BlockSpec((1, rows, dh), lambda bi, s, pt: (bi, 0, 0)),
                  pl.BlockSpec((rows, 1), lambda bi, s, pt: (0, 0)),
                  pl.BlockSpec((1, page * n_heads, dh), page_map),
                  pl.BlockSpec((1, page * n_heads, dh), page_map),
                  pl.BlockSpec((1, page * n_heads, dh), lambda bi, s, pt: (bi, 0, 0)),
                  pl.BlockSpec((1, page * n_heads, dh), lambda bi, s, pt: (bi, 0, 0))],
        out_specs=pl.BlockSpec((1, rows, dh), lambda bi, s, pt: (bi, 0, 0)),
        scratch_shapes=[pltpu.VMEM((rows, page), F32), pltpu.VMEM((rows, page), BF16),
                        pltpu.VMEM((rows, dh), F32), pltpu.VMEM((rows, 1), F32)],
    )
    return pl.pallas_call(
        functools.partial(_attn_paged_kernel, n_heads=n_heads, page=page, qpad=qpad),
        grid_spec=grid_spec,
        out_shape=jax.ShapeDtypeStruct((b, rows, dh), F32),
        compiler_params=_cparams(("parallel", "arbitrary")),
        name="stick_attention_paged",
    )(page_table, q_rows, bias_col, cache_k, cache_v, k_new, v_new)


def kernel(x_prompt, x_sample, c_prompt, c_sample, cache_k, cache_v, page_table, ada_w, ada_b, ln1_g, ln1_b, ln2_g, ln2_b, a_w_in, a_b_in, a_vnorm_g, a_vnorm_b, a_w_s, a_b_s, a_w_out, kv_w, b_w_q, b_w_o, b_logit_bias, moe_router_group_w, moe_router_group_b, moe_router_expert_w, moe_router_expert_b, moe_w_gate, moe_w_up, moe_w_down):
    bp, sp, d = x_prompt.shape
    bs, ss, _ = x_sample.shape
    depth = ada_w.shape[0]
    n_a = a_w_in.shape[0]
    d_a = a_w_out.shape[1]
    chunk = a_w_s.shape[-1]
    n_pool, page, n_heads, dh = cache_k.shape
    hd = n_heads * dh
    n_groups, n_per = moe_router_expert_b.shape[1:]
    n_experts = n_groups * n_per
    alpha = (2 * depth) ** 0.25
    assert n_groups + n_experts <= ROUTE_WIDTH and ss <= SUBLANES and sp % chunk == 0
    np_tok, ns_tok = bp * sp, bs * ss
    ts_p = min(512, sp)
    tiles_p = sp // ts_p
    assert np_tok % ns_tok == 0

    c_rows = -(-(bp + bs) // SUBLANES) * SUBLANES
    c_all = jnp.concatenate([c_prompt, c_sample, jnp.zeros((c_rows - bp - bs, d), F32)], axis=0)
    mod = _modulation(c_all, ada_w, ada_b)

    def mods(l, group):
        m = mod[l].reshape(c_rows, 6, d)
        if group == 0:
            return [m[:bp, j].reshape(bp, 1, d) for j in range(6)]
        return [jnp.repeat(m[bp:bp + bs, j], ss, axis=0).reshape(1, ns_tok, d) for j in range(6)]

    groups = [dict(x=x_prompt.reshape(np_tok, d), ts=ts_p, tiles=tiles_p, per_token=False, n=np_tok),
              dict(x=x_sample.reshape(ns_tok, d), ts=ns_tok, tiles=1, per_token=True, n=ns_tok)]
    tm = 128
    chunk_v = []
    k_out, v_out = [None, None], [None, None]

    for l in range(depth):
        w_route = jnp.concatenate(
            [moe_router_group_w[l],
             jnp.transpose(moe_router_expert_w[l], (1, 0, 2)).reshape(d, n_experts),
             jnp.zeros((d, ROUTE_WIDTH - n_groups - n_experts), F32)], axis=1).astype(BF16)
        b_route = jnp.concatenate(
            [moe_router_group_b[l], moe_router_expert_b[l].reshape(-1),
             jnp.zeros((ROUTE_WIDTH - n_groups - n_experts,), F32)]).reshape(1, ROUTE_WIDTH)
        x1s, h2s, routes, g2s = [], [], [], []
        for gi, grp in enumerate(groups):
            sh1, sc1, g1, sh2, sc2, g2 = mods(l, gi)
            x, ts, tiles, per_token = grp["x"], grp["ts"], grp["tiles"], grp["per_token"]
            if l < n_a:
                u, v_raw = _projection(x, a_w_in[l].astype(BF16), (0, d_a), d_a, (F32, F32), ts=ts,
                                       tiles_per_seq=tiles, mod=(sc1, sh1, per_token),
                                       bias=a_b_in[l].reshape(1, 2 * d_a), act=True, name="gmlp_in")
                bs_t = jnp.transpose(a_b_s[l])
                if gi == 0:
                    n_chunks = np_tok // chunk
                    gated, = _spatial_gate(v_raw.reshape(n_chunks, chunk, d_a), u.reshape(n_chunks, chunk, d_a),
                                           a_vnorm_g[l].reshape(1, d_a), a_vnorm_b[l].reshape(1, d_a),
                                           a_w_s[l], bs_t, emit_vn=False)
                    a_mix = gated.reshape(np_tok, d_a)
                else:
                    pad = ((0, 0), (0, SAMPLE_ROWS - ss), (0, 0))
                    gated, vn = _spatial_gate(jnp.pad(v_raw.reshape(bs, ss, d_a), pad),
                                              jnp.pad(u.reshape(bs, ss, d_a), pad),
                                              a_vnorm_g[l].reshape(1, d_a), a_vnorm_b[l].reshape(1, d_a),
                                              a_w_s[l][:, :SAMPLE_ROWS, :SAMPLE_ROWS], bs_t[:SAMPLE_ROWS], emit_vn=True)
                    a_mix = gated[:, :ss].reshape(ns_tok, d_a)
                    chunk_v.append(vn[:, :ss])
                w_mix = a_w_out[l].astype(BF16)
            else:
                j = l - n_a
                if k_out[gi] is None:
                    k_out[gi], v_out[gi] = _projection(x, kv_w.astype(BF16), (0, hd), hd, (F32, F32), ts=ts,
                                                       tiles_per_seq=tiles, name="kv_proj")
                q, = _projection(x, b_w_q[j].astype(BF16), (0,), hd, (BF16,), ts=ts, tiles_per_seq=tiles,
                                 mod=(sc1, sh1, per_token), name="q_proj")
                if gi == 0:
                    o = _attention_prompt(q.reshape(bp, sp, hd), k_out[gi].reshape(bp, sp, hd),
                                          v_out[gi].reshape(bp, sp, hd), b_logit_bias[j], n_heads=n_heads)
                    a_mix = o.reshape(np_tok, hd)
                else:
                    qpad = SUBLANES
                    q_rows = jnp.pad(jnp.transpose(q.reshape(bs, ss, n_heads, dh), (0, 2, 1, 3)),
                                     ((0, 0), (0, 0), (0, qpad - ss), (0, 0))).reshape(bs, n_heads * qpad, dh)

                    def new_page(t):
                        t = jnp.pad(t.reshape(bs, ss, n_heads, dh), ((0, 0), (0, page - ss), (0, 0), (0, 0)))
                        return t.reshape(bs, page * n_heads, dh)

                    o = _attention_paged(q_rows, jnp.repeat(b_logit_bias[j], qpad).reshape(n_heads * qpad, 1),
                                         cache_k.reshape(n_pool, page * n_heads, dh),
                                         cache_v.reshape(n_pool, page * n_heads, dh),
                                         new_page(k_out[gi]), new_page(v_out[gi]), page_table,
                                         n_heads=n_heads, qpad=qpad)
                    o = jnp.transpose(o.reshape(bs, n_heads, qpad, dh)[:, :, :ss], (0, 2, 1, 3))
                    a_mix = o.reshape(ns_tok, hd).astype(BF16)
                w_mix = b_w_o[j].astype(BF16)
            x1, h2, route = _out_projection(a_mix, w_mix, x, g1, ln1_g[l].reshape(1, d), ln1_b[l].reshape(1, d),
                                            sc2, sh2, w_route, b_route, per_token=per_token, ts=ts,
                                            tiles_per_seq=tiles, alpha=alpha, n_groups=n_groups, n_per=n_per)
            x1s.append(x1)
            h2s.append(h2)
            routes.append(route)
            g2s.append(g2)

        h2_all = jnp.concatenate(h2s, axis=0)
        be, nv, row_dst, row_gate = _dispatch(jnp.concatenate(routes, axis=0), n_experts, tm)
        out2 = _moe_experts(h2_all, be, nv, row_dst, row_gate, moe_w_gate, moe_w_up, moe_w_down, l, tm=tm)
        out2 = out2.reshape(np_tok + ns_tok, TOP_K * d)
        offsets = [0, np_tok // ns_tok]
        for gi, grp in enumerate(groups):
            grp["x"] = _combine(x1s[gi], out2, offsets[gi], g2s[gi], ln2_g[l].reshape(1, d), ln2_b[l].reshape(1, d),
                                per_token=grp["per_token"], ts=grp["ts"], tiles_per_seq=grp["tiles"], alpha=alpha)

    y_prompt = groups[0]["x"].reshape(bp, sp, d)
    y_sample = groups[1]["x"].reshape(bs, ss, d)
    return (y_prompt, y_sample,
            k_out[0].reshape(bp, sp, n_heads, dh), v_out[0].reshape(bp, sp, n_heads, dh),
            k_out[1].reshape(bs, ss, n_heads, dh), v_out[1].reshape(bs, ss, n_heads, dh),
            jnp.stack(chunk_v))
```

```python
import functools
import math

import jax
import jax.numpy as jnp
from jax import lax
from jax.experimental import pallas as pl
from jax.experimental.pallas import tpu as pltpu

F32 = jnp.float32
BF16 = jnp.bfloat16
LN_EPS = 1e-5
V7X_VMEM_LIMIT_BYTES = 56 * 1024 * 1024
LANES = 128
SUBLANES = 8
TOP_K = 2
ROUTE_WIDTH = LANES
EPILOGUE_ROWS = 128
DMA_UNROLL = 8
PAGES_PER_STEP = 4
SAMPLE_ROWS = 2 * SUBLANES


def _cparams(semantics):
    return pltpu.CompilerParams(dimension_semantics=semantics,
                                vmem_limit_bytes=V7X_VMEM_LIMIT_BYTES)


def _layer_norm(x, g, b):
    mu = jnp.mean(x, axis=-1, keepdims=True)
    xc = x - mu
    var = jnp.mean(xc * xc, axis=-1, keepdims=True)
    return xc * lax.rsqrt(var + LN_EPS) * g + b


def _gelu_tanh(x):
    c = math.sqrt(2.0 / math.pi)
    return x * (0.5 * (1.0 + jnp.tanh(c * (x + 0.044715 * (x * x * x)))))


def _log_sigmoid(z):
    return jnp.minimum(z, 0.0) - jnp.log(1.0 + jnp.exp(-jnp.abs(z)))


def _mod_spec(per_token, ts, tiles_per_seq, d):
    if per_token:
        return pl.BlockSpec((1, ts, d), lambda i, *_: (0, i, 0))
    return pl.BlockSpec((1, 1, d), lambda i, *_: (i // tiles_per_seq, 0, 0))


def _mod_kernel(c_ref, w_ref, b_ref, o_ref):
    c = c_ref[...].astype(BF16)
    w = w_ref[0].astype(BF16)
    o_ref[0] = jnp.dot(c, w, preferred_element_type=F32) + b_ref[0]


def _modulation(c_all, ada_w, ada_b):
    depth, d, d6 = ada_w.shape
    rows = c_all.shape[0]
    tn = 1024 if d6 % 1024 == 0 else d6
    return pl.pallas_call(
        _mod_kernel,
        grid=(depth, d6 // tn),
        in_specs=[pl.BlockSpec((rows, d), lambda l, n: (0, 0)),
                  pl.BlockSpec((1, d, tn), lambda l, n: (l, 0, n)),
                  pl.BlockSpec((1, 1, tn), lambda l, n: (l, 0, n))],
        out_specs=pl.BlockSpec((1, rows, tn), lambda l, n: (l, 0, n)),
        out_shape=jax.ShapeDtypeStruct((depth, rows, d6), F32),
        compiler_params=_cparams(("arbitrary", "arbitrary")),
        name="adaln_modulation",
    )(c_all, ada_w, ada_b.reshape(depth, 1, d6))


def _proj_kernel(*refs, n_out, modulate, has_bias, act, out_scale):
    refs = list(refs)
    x_ref = refs.pop(0)
    sc_ref = refs.pop(0) if modulate else None
    sh_ref = refs.pop(0) if modulate else None
    w_refs = [refs.pop(0) for _ in range(n_out)]
    b_refs = [refs.pop(0) for _ in range(n_out)] if has_bias else [None] * n_out
    o_refs = [refs.pop(0) for _ in range(n_out)]
    h_ref = refs.pop(0)

    @pl.when(pl.program_id(1) == 0)
    def _():
        x = x_ref[...]
        if modulate:
            x = x * (1.0 + sc_ref[0]) + sh_ref[0]
        h_ref[...] = x.astype(BF16)

    h = h_ref[...]
    for w_ref, b_ref, o_ref in zip(w_refs, b_refs, o_refs):
        z = jnp.dot(h, w_ref[...], preferred_element_type=F32)
        if has_bias:
            z = z + b_ref[...]
        if act:
            z = _gelu_tanh(z)
        if out_scale != 1.0:
            z = z * out_scale
        o_ref[...] = z.astype(o_ref.dtype)


def _projection(x, w, col_starts, width, out_dtypes, *, ts, tiles_per_seq, mod=None, bias=None,
                act=False, out_scale=1.0, name="projection"):
    n, d = x.shape
    tn = min(512, width)
    n_out = len(col_starts)
    modulate = mod is not None
    in_specs = [pl.BlockSpec((ts, d), lambda i, j: (i, 0))]
    args = [x]
    if modulate:
        sc, sh, per_token = mod
        in_specs += [_mod_spec(per_token, ts, tiles_per_seq, d)] * 2
        args += [sc, sh]
    for c0 in col_starts:
        in_specs.append(pl.BlockSpec((d, tn), functools.partial(lambda i, j, o: (0, j + o), o=c0 // tn)))
        args.append(w)
    if bias is not None:
        for c0 in col_starts:
            in_specs.append(pl.BlockSpec((1, tn), functools.partial(lambda i, j, o: (0, j + o), o=c0 // tn)))
            args.append(bias)
    return pl.pallas_call(
        functools.partial(_proj_kernel, n_out=n_out, modulate=modulate, has_bias=bias is not None, act=act,
                          out_scale=out_scale),
        grid=(n // ts, width // tn),
        in_specs=in_specs,
        out_specs=[pl.BlockSpec((ts, tn), lambda i, j: (i, j))] * n_out,
        out_shape=[jax.ShapeDtypeStruct((n, width), dt) for dt in out_dtypes],
        scratch_shapes=[pltpu.VMEM((ts, d), BF16)],
        compiler_params=_cparams(("parallel", "arbitrary")),
        name=name,
    )(*args)


def _gate_kernel(v_ref, u_ref, g_ref, b_ref, ws_ref, bs_ref, o_ref, *maybe_vn_ref, n_groups):
    v = v_ref[0]
    vn = _layer_norm(v, g_ref[...], b_ref[...])
    if maybe_vn_ref:
        maybe_vn_ref[0][0] = vn
    vb = vn.astype(BF16)
    rows, d_a = v.shape
    gd = d_a // n_groups
    causal = (lax.broadcasted_iota(jnp.int32, (rows, rows), 0)
              >= lax.broadcasted_iota(jnp.int32, (rows, rows), 1))
    for g in range(n_groups):
        w = jnp.where(causal, ws_ref[g], 0.0).astype(BF16)
        s = jnp.dot(w, vb[:, g * gd:(g + 1) * gd], preferred_element_type=F32) + bs_ref[:, g:g + 1]
        o_ref[0, :, g * gd:(g + 1) * gd] = (u_ref[0, :, g * gd:(g + 1) * gd].astype(F32) * s).astype(o_ref.dtype)


def _spatial_gate(v_raw, u, vn_g, vn_b, w_s, b_s_t, *, emit_vn):
    c, r, d_a = v_raw.shape
    n_groups = w_s.shape[0]
    blk = pl.BlockSpec((1, r, d_a), lambda i: (i, 0, 0))
    out_shape = [jax.ShapeDtypeStruct((c, r, d_a), BF16)]
    out_specs = [blk]
    if emit_vn:
        out_shape.append(jax.ShapeDtypeStruct((c, r, d_a), F32))
        out_specs.append(blk)
    return pl.pallas_call(
        functools.partial(_gate_kernel, n_groups=n_groups),
        grid=(c,),
        in_specs=[blk, blk,
                  pl.BlockSpec((1, d_a), lambda i: (0, 0)),
                  pl.BlockSpec((1, d_a), lambda i: (0, 0)),
                  pl.BlockSpec((n_groups, r, r), lambda i: (0, 0, 0)),
                  pl.BlockSpec((r, n_groups), lambda i: (0, 0))],
        out_specs=out_specs,
        out_shape=out_shape,
        compiler_params=_cparams(("parallel",)),
        name="spatial_gate",
    )(v_raw, u, vn_g, vn_b, w_s, b_s_t)


def _route(logits, n_groups, n_per):
    lane = lax.broadcasted_iota(jnp.int32, logits.shape, 1)
    lane_f = lane.astype(F32)
    neg = -jnp.inf
    big = float(2 * ROUTE_WIDTH)
    gl = jnp.where(lane < n_groups, logits, neg)
    gmax = jnp.max(gl, axis=-1, keepdims=True)
    g_sel = jnp.min(jnp.where(gl == gmax, lane_f, big), axis=-1, keepdims=True)
    g_weight = 1.0 / jnp.sum(jnp.exp(gl - gmax), axis=-1, keepdims=True)
    e_lane = lane_f - float(n_groups)
    in_group = (e_lane >= g_sel * n_per) & (e_lane < (g_sel + 1.0) * n_per)
    el = jnp.where(in_group, logits, neg)
    v1 = jnp.max(el, axis=-1, keepdims=True)
    i1 = jnp.min(jnp.where(el == v1, lane_f, big), axis=-1, keepdims=True)
    el2 = jnp.where(lane_f == i1, neg, el)
    v2 = jnp.max(el2, axis=-1, keepdims=True)
    i2 = jnp.min(jnp.where(el2 == v2, lane_f, big), axis=-1, keepdims=True)
    e2 = jnp.exp(v2 - v1)
    gate1 = g_weight * (1.0 / (1.0 + e2))
    gate2 = g_weight * (e2 / (1.0 + e2))
    id1 = i1 - float(n_groups)
    id2 = i2 - float(n_groups)
    return jnp.where(lane == 0, id1,
                     jnp.where(lane == 1, id2,
                               jnp.where(lane == 2, gate1,
                                         jnp.where(lane == 3, gate2, 0.0))))


def _outproj_kernel(a_ref, w_ref, x_ref, g1_ref, lng_ref, lnb_ref, sc2_ref, sh2_ref, wr_ref, br_ref,
                    x1_ref, h2_ref, rt_ref, acc_ref, *, alpha, n_groups, n_per):
    k = pl.program_id(1)

    @pl.when(k == 0)
    def _():
        acc_ref[...] = jnp.zeros_like(acc_ref)

    acc_ref[...] += jnp.dot(a_ref[...], w_ref[...], preferred_element_type=F32)

    @pl.when(k == pl.num_programs(1) - 1)
    def _():
        rows = min(EPILOGUE_ROWS, acc_ref.shape[0])

        def slab(s, c):
            r = pl.ds(pl.multiple_of(s * rows, rows), rows)
            g1 = g1_ref[0] if g1_ref.shape[1] == 1 else g1_ref[0, r, :]
            sc2 = sc2_ref[0] if sc2_ref.shape[1] == 1 else sc2_ref[0, r, :]
            sh2 = sh2_ref[0] if sh2_ref.shape[1] == 1 else sh2_ref[0, r, :]
            t = alpha * x_ref[r, :] + (1.0 + g1) * acc_ref[r, :]
            x1 = _layer_norm(t, lng_ref[...], lnb_ref[...])
            x1_ref[r, :] = x1
            h2 = x1 * (1.0 + sc2) + sh2
            h2_ref[r, :] = h2
            logits = jnp.dot(h2.astype(BF16), wr_ref[...], preferred_element_type=F32) + br_ref[...]
            rt_ref[r, :] = _route(logits, n_groups, n_per)
            return c

        lax.fori_loop(0, acc_ref.shape[0] // rows, slab, 0)


def _out_projection(a, w, x, g1, ln_g, ln_b, sc2, sh2, w_route, b_route, *, per_token, ts, tiles_per_seq,
                    alpha, n_groups, n_per):
    n, kdim = a.shape
    d = w.shape[1]
    tk = min(512, kdim)
    mspec = _mod_spec(per_token, ts, tiles_per_seq, d)
    vec = pl.BlockSpec((1, d), lambda i, k: (0, 0))
    tile = pl.BlockSpec((ts, d), lambda i, k: (i, 0))
    return pl.pallas_call(
        functools.partial(_outproj_kernel, alpha=alpha, n_groups=n_groups, n_per=n_per),
        grid=(n // ts, kdim // tk),
        in_specs=[pl.BlockSpec((ts, tk), lambda i, k: (i, k)),
                  pl.BlockSpec((tk, d), lambda i, k: (k, 0)),
                  tile, mspec, vec, vec, mspec, mspec,
                  pl.BlockSpec((d, ROUTE_WIDTH), lambda i, k: (0, 0)),
                  pl.BlockSpec((1, ROUTE_WIDTH), lambda i, k: (0, 0))],
        out_specs=[tile, tile, pl.BlockSpec((ts, ROUTE_WIDTH), lambda i, k: (i, 0))],
        out_shape=[jax.ShapeDtypeStruct((n, d), F32), jax.ShapeDtypeStruct((n, d), F32),
                   jax.ShapeDtypeStruct((n, ROUTE_WIDTH), F32)],
        scratch_shapes=[pltpu.VMEM((ts, d), F32)],
        compiler_params=_cparams(("parallel", "arbitrary")),
        name="mixer_out_ln_route",
    )(a, w, x, g1, ln_g, ln_b, sc2, sh2, w_route, b_route)


def _moe_kernel(be_ref, nv_ref, rd_ref, h_hbm, wg_ref, wu_ref, wd_ref, gate_ref, out_hbm,
                xbuf, obuf, wgb, wub, wdb, gsem, ssem, *, tm, slot_stride):
    i = pl.program_id(0)
    nb = pl.num_programs(0)
    slot = i % 2

    def gather_copy(blk, s, r):
        tok = rd_ref[blk * tm + r] >> 1
        return pltpu.make_async_copy(h_hbm.at[pl.ds(tok, 1)], xbuf.at[s, pl.ds(r, 1)], gsem.at[s])

    def scatter_copy(blk, s, r):
        a = rd_ref[blk * tm + r]
        dst = (a & 1) * slot_stride + (a >> 1)
        return pltpu.make_async_copy(obuf.at[s, pl.ds(r, 1)], out_hbm.at[pl.ds(dst, 1)], ssem.at[s])

    def for_rows(n, fn):
        n_groups = n // DMA_UNROLL

        def group(gidx, c):
            for u in range(DMA_UNROLL):
                fn(gidx * DMA_UNROLL + u)
            return c

        def tail(r, c):
            fn(r)
            return c

        lax.fori_loop(0, n_groups, group, 0)
        lax.fori_loop(n_groups * DMA_UNROLL, n, tail, 0)

    def start_gather(blk, s):
        for_rows(nv_ref[blk], lambda r: gather_copy(blk, s, r).start())

    def wait_rows(n, s, gather):
        def copy(rows):
            if gather:
                return pltpu.make_async_copy(h_hbm.at[rows], xbuf.at[s, rows], gsem.at[s])
            return pltpu.make_async_copy(obuf.at[s, rows], out_hbm.at[rows], ssem.at[s])

        n_aligned = pl.multiple_of((n // SUBLANES) * SUBLANES, SUBLANES)

        @pl.when(n_aligned > 0)
        def _():
            copy(pl.ds(0, n_aligned)).wait()

        def tail(r, c):
            copy(pl.ds(0, 1)).wait()
            return c

        lax.fori_loop(n_aligned, n, tail, 0)

    @pl.when(i == 0)
    def _():
        xbuf[...] = jnp.zeros_like(xbuf)
        start_gather(0, 0)

    nxt = jnp.minimum(i + 1, nb - 1)

    @pl.when(i + 1 < nb)
    def _():
        start_gather(nxt, 1 - slot)

    @pl.when(i >= 2)
    def _():
        wait_rows(nv_ref[jnp.maximum(i - 2, 0)], slot, False)

    @pl.when(nv_ref[i] > 0)
    def _():
        wait_rows(nv_ref[i], slot, True)
        prev = jnp.maximum(i - 1, 0)

        @pl.when((i == 0) | (be_ref[i] != be_ref[prev]))
        def _():
            wgb[...] = wg_ref[0, 0].astype(BF16)
            wub[...] = wu_ref[0, 0].astype(BF16)
            wdb[...] = wd_ref[0, 0].astype(BF16)

        x = xbuf[slot].astype(BF16)
        g = jnp.dot(x, wgb[...], preferred_element_type=F32)
        u = jnp.dot(x, wub[...], preferred_element_type=F32)
        mid = (g * jax.nn.sigmoid(g) * u).astype(BF16)
        o = jnp.dot(mid, wdb[...], preferred_element_type=F32)
        obuf[slot] = o * gate_ref[...]
        for_rows(nv_ref[i], lambda r: scatter_copy(i, slot, r).start())

    @pl.when(i == nb - 1)
    def _():
        @pl.when(i >= 1)
        def _():
            wait_rows(nv_ref[jnp.maximum(i - 1, 0)], 1 - slot, False)
        wait_rows(nv_ref[i], slot, False)


def _moe_experts(h2, blk_expert, blk_nvalid, row_dst, row_gate, w_gate, w_up, w_down, layer, *, tm, slot_stride):
    n, d = h2.shape
    _, n_experts, _, de = w_gate.shape
    nb = blk_expert.shape[0]
    grid_spec = pltpu.PrefetchScalarGridSpec(
        num_scalar_prefetch=3,
        grid=(nb,),
        in_specs=[pl.BlockSpec(memory_space=pl.ANY),
                  pl.BlockSpec((1, 1, d, de), lambda i, be, nv, rd: (layer, be[i], 0, 0)),
                  pl.BlockSpec((1, 1, d, de), lambda i, be, nv, rd: (layer, be[i], 0, 0)),
                  pl.BlockSpec((1, 1, de, d), lambda i, be, nv, rd: (layer, be[i], 0, 0)),
                  pl.BlockSpec((tm, 1), lambda i, be, nv, rd: (i, 0))],
        out_specs=pl.BlockSpec(memory_space=pl.ANY),
        scratch_shapes=[pltpu.VMEM((2, tm, d), F32), pltpu.VMEM((2, tm, d), F32),
                        pltpu.VMEM((d, de), BF16), pltpu.VMEM((d, de), BF16), pltpu.VMEM((de, d), BF16),
                        pltpu.SemaphoreType.DMA((2,)), pltpu.SemaphoreType.DMA((2,))],
    )
    return pl.pallas_call(
        functools.partial(_moe_kernel, tm=tm, slot_stride=slot_stride),
        grid_spec=grid_spec,
        out_shape=jax.ShapeDtypeStruct((TOP_K * slot_stride, d), F32),
        compiler_params=_cparams(("arbitrary",)),
        name="moe_experts",
    )(blk_expert, blk_nvalid, row_dst, h2, w_gate, w_up, w_down, row_gate)


def _dispatch(route, n_experts, tm):
    n = route.shape[0]
    eid = route[:, :TOP_K].astype(jnp.int32).reshape(-1)
    gates = route[:, TOP_K:2 * TOP_K].reshape(-1)
    n_assign = n * TOP_K
    nb = -(-n_assign // tm) + n_experts
    _, order, gates_sorted = lax.sort((eid, jnp.arange(n_assign, dtype=jnp.int32), gates), num_keys=1)
    experts = jnp.arange(n_experts, dtype=jnp.int32)
    starts = jnp.sum(eid[None, :] < experts[:, None], axis=1, dtype=jnp.int32)
    counts = jnp.sum(eid[None, :] == experts[:, None], axis=1, dtype=jnp.int32)
    nblk = (counts + tm - 1) // tm
    blk_end = jnp.cumsum(nblk)
    blk_start = blk_end - nblk
    blk = jnp.arange(nb, dtype=jnp.int32)
    n_used = blk_end[-1]
    be = jnp.minimum(jnp.sum(blk_end[None, :] <= blk[:, None], axis=1, dtype=jnp.int32), n_experts - 1)
    mine = be[:, None] == experts[None, :]

    def of_expert(per_expert):
        return jnp.sum(jnp.where(mine, per_expert[None, :], 0), axis=1, dtype=jnp.int32)

    used = blk < n_used
    first_row = (blk - of_expert(blk_start)) * tm
    nv = jnp.where(used, jnp.clip(of_expert(counts) - first_row, 0, tm), 0).astype(jnp.int32)
    r = jnp.arange(tm, dtype=jnp.int32)[None, :]
    src = jnp.clip((of_expert(starts) + first_row)[:, None] + r, 0, n_assign - 1)
    valid = r < nv[:, None]
    row_dst = jnp.where(valid, order[src], 0).reshape(nb * tm)
    row_gate = jnp.where(valid, gates_sorted[src], 0.0).reshape(nb * tm, 1)
    last_e = jnp.max(jnp.where(used, be, 0))
    be = jnp.where(used, be, last_e)
    return be, nv, row_dst, row_gate


def _combine_kernel(x1_ref, oa_ref, ob_ref, g2_ref, lng_ref, lnb_ref, x2_ref, *, alpha):
    y = oa_ref[...] + ob_ref[...]
    t = alpha * x1_ref[...] + (1.0 + g2_ref[0]) * y
    x2_ref[...] = _layer_norm(t, lng_ref[...], lnb_ref[...])


def _combine(x1, out2, first_row, slot_stride, g2, ln_g, ln_b, *, per_token, ts, tiles_per_seq, alpha):
    n, d = x1.shape
    vec = pl.BlockSpec((1, d), lambda i: (0, 0))
    tile = pl.BlockSpec((ts, d), lambda i: (i, 0))
    off_a = first_row // ts
    off_b = (first_row + slot_stride) // ts
    return pl.pallas_call(
        functools.partial(_combine_kernel, alpha=alpha),
        grid=(n // ts,),
        in_specs=[tile,
                  pl.BlockSpec((ts, d), lambda i: (i + off_a, 0)),
                  pl.BlockSpec((ts, d), lambda i: (i + off_b, 0)),
                  _mod_spec(per_token, ts, tiles_per_seq, d), vec, vec],
        out_specs=tile,
        out_shape=jax.ShapeDtypeStruct((n, d), F32),
        compiler_params=_cparams(("parallel",)),
        name="moe_combine_ln",
    )(x1, out2, out2, g2, ln_g, ln_b)


def _split2(x):
    hi = x.astype(BF16)
    return hi, (x - hi.astype(F32)).astype(BF16)


def _stick_terms(z, vis, excl_ones):
    lb = _log_sigmoid(z)
    ls = lb - z
    if vis is not None:
        ls = jnp.where(vis, ls, 0.0)
    hi, lo = _split2(ls)
    later = jnp.dot(hi, excl_ones, preferred_element_type=F32) + jnp.dot(lo, excl_ones, preferred_element_type=F32)
    return lb, later, jnp.sum(ls, axis=-1, keepdims=True)


def _stick_weights(lb, later, run, vis):
    a = jnp.exp(lb + (later + run))
    if vis is not None:
        a = jnp.where(vis, a, 0.0)
    return a


def _stick_block(z, vis, excl_ones, run):
    lb, later, total = _stick_terms(z, vis, excl_ones)
    return _stick_weights(lb, later, run, vis), run + total


def _excl_ones(tk):
    j = lax.broadcasted_iota(jnp.int32, (tk, tk), 0)
    s = lax.broadcasted_iota(jnp.int32, (tk, tk), 1)
    return jnp.where(j > s, 1.0, 0.0).astype(BF16)


def _attn_kernel(bias_ref, q_ref, k_ref, v_ref, o_ref, *, t, dh, heads):
    scale = dh ** -0.5
    hg = pl.program_id(1)
    qi = pl.program_id(2)
    ones = _excl_ones(t)
    below = (lax.broadcasted_iota(jnp.int32, (t, t), 1) < lax.broadcasted_iota(jnp.int32, (t, t), 0))

    def blocks(starts, carry, vis):
        out = []
        for hh in range(heads):
            acc, run = carry[hh]
            lanes = slice(hh * dh, (hh + 1) * dh)
            terms = []
            for start in starts:
                k = k_ref[0, pl.ds(start, t), lanes].astype(BF16)
                z = lax.dot_general(q_ref[0, :, lanes], k, (((1,), (1,)), ((), ())),
                                    preferred_element_type=F32) * scale + bias_ref[hg * heads + hh]
                terms.append(_stick_terms(z, vis, ones))
            for start, (lb, later, total) in zip(starts, terms):
                a = _stick_weights(lb, later, run, vis)
                run = run + total
                v = v_ref[0, pl.ds(start, t), lanes].astype(BF16)
                acc = acc + jnp.dot(a.astype(BF16), v, preferred_element_type=F32)
            out.append((acc, run))
        return tuple(out)

    def pair(j, c):
        newer = pl.multiple_of((qi - 1 - 2 * j) * t, t)
        return blocks([newer, pl.multiple_of(newer - t, t)], c, None)

    init = tuple((jnp.zeros((t, dh), F32), jnp.zeros((t, 1), F32)) for _ in range(heads))
    carry = blocks([pl.multiple_of(qi * t, t)], init, below)
    carry = lax.fori_loop(0, qi // 2, pair, carry)
    carry = lax.fori_loop(0, qi % 2, lambda j, c: blocks([0], c, None), carry)
    for hh in range(heads):
        o_ref[0, :, hh * dh:(hh + 1) * dh] = carry[hh][0].astype(o_ref.dtype)


def _attention_prompt(q, k, v, logit_bias, *, n_heads):
    b, s, hd = q.shape
    dh = hd // n_heads
    t = min(256, s)
    heads = 2 if n_heads % 2 == 0 else 1
    w = heads * dh
    return pl.pallas_call(
        functools.partial(_attn_kernel, t=t, dh=dh, heads=heads),
        grid=(b, n_heads // heads, s // t),
        in_specs=[pl.BlockSpec(memory_space=pltpu.SMEM),
                  pl.BlockSpec((1, t, w), lambda bi, h, qi: (bi, qi, h)),
                  pl.BlockSpec((1, s, w), lambda bi, h, qi: (bi, 0, h)),
                  pl.BlockSpec((1, s, w), lambda bi, h, qi: (bi, 0, h))],
        out_specs=pl.BlockSpec((1, t, w), lambda bi, h, qi: (bi, qi, h)),
        out_shape=jax.ShapeDtypeStruct((b, s, hd), BF16),
        compiler_params=_cparams(("parallel", "parallel", "arbitrary")),
        name="stick_attention_prompt",
    )(logit_bias, q, k, v)


def _attn_paged_kernel(pt_ref, q_ref, bias_ref, own_ref, pick_ref, spread_ref, *rest, page, nq, pps):
    k_refs, v_refs = rest[:pps], rest[pps:2 * pps]
    kn_ref, vn_ref, o_ref, acc_ref, run_ref = rest[2 * pps:]
    step = pl.program_id(1)
    scale = q_ref.shape[-1] ** -0.5

    @pl.when(step == 0)
    def _():
        acc_ref[...] = jnp.zeros_like(acc_ref)
        run_ref[...] = jnp.zeros_like(run_ref)

    def process(pages, new_keys):
        own = own_ref[...]
        ones = _excl_ones(page)
        terms = []
        for k_ref, _ in pages:
            s_all = lax.dot_general(q_ref[0], k_ref[0].astype(BF16), (((1,), (1,)), ((), ())),
                                    preferred_element_type=F32)
            hi, lo = _split2(s_all * own)
            z = (jnp.dot(hi, pick_ref[...], preferred_element_type=F32)
                 + jnp.dot(lo, pick_ref[...], preferred_element_type=F32)) * scale + bias_ref[...]
            vis = None
            if new_keys:
                t = lax.broadcasted_iota(jnp.int32, z.shape, 0) % nq
                vis = lax.broadcasted_iota(jnp.int32, z.shape, 1) < t
            terms.append(_stick_terms(z, vis, ones) + (vis,))
        run = run_ref[...]
        acc = acc_ref[...]
        for (lb, later, total, vis), (_, v_ref) in zip(terms, pages):
            a = _stick_weights(lb, later, run, vis)
            run = run + total
            a_all = (jnp.dot(a.astype(BF16), spread_ref[...], preferred_element_type=F32) * own).astype(BF16)
            acc = acc + jnp.dot(a_all, v_ref[0].astype(BF16), preferred_element_type=F32)
        run_ref[...] = run
        acc_ref[...] = acc

    @pl.when(step == 0)
    def _():
        process([(kn_ref, vn_ref)], True)

    @pl.when(step > 0)
    def _():
        process(list(zip(k_refs, v_refs)), False)

    @pl.when(step == pl.num_programs(1) - 1)
    def _():
        o_ref[0] = acc_ref[...]


def _attention_paged(q_rows, bias_col, cache_k, cache_v, k_new, v_new, page_table, *, n_heads, nq):
    b, rows, dh = q_rows.shape
    n_pages = page_table.shape[1]
    cols = cache_k.shape[1]
    page = cols // n_heads
    col = jnp.arange(cols, dtype=jnp.int32)
    own = (col[None, :] % n_heads == jnp.arange(rows, dtype=jnp.int32)[:, None] // nq).astype(F32)
    pick = (col[:, None] // n_heads == jnp.arange(page, dtype=jnp.int32)[None, :]).astype(BF16)

    pps = next(p for p in (PAGES_PER_STEP, 2, 1) if n_pages % p == 0)

    def page_spec(j):
        return pl.BlockSpec(
            (1, cols, dh), lambda bi, s, pt: (pt[bi, n_pages - 1 - (jnp.maximum(s, 1) - 1) * pps - j], 0, 0))

    def const(shape):
        return pl.BlockSpec(shape, lambda bi, s, pt: (0,) * len(shape))

    grid_spec = pltpu.PrefetchScalarGridSpec(
        num_scalar_prefetch=1,
        grid=(b, n_pages // pps + 1),
        in_specs=[pl.BlockSpec((1, rows, dh), lambda bi, s, pt: (bi, 0, 0)),
                  const((rows, 1)), const((rows, cols)), const((cols, page)), const((page, cols))]
                 + [page_spec(j) for j in range(pps)] * 2
                 + [pl.BlockSpec((1, cols, dh), lambda bi, s, pt: (bi, 0, 0))] * 2,
        out_specs=pl.BlockSpec((1, rows, dh), lambda bi, s, pt: (bi, 0, 0)),
        scratch_shapes=[pltpu.VMEM((rows, dh), F32), pltpu.VMEM((rows, 1), F32)],
    )
    return pl.pallas_call(
        functools.partial(_attn_paged_kernel, page=page, nq=nq, pps=pps),
        grid_spec=grid_spec,
        out_shape=jax.ShapeDtypeStruct((b, rows, dh), F32),
        compiler_params=_cparams(("parallel", "arbitrary")),
        name="stick_attention_paged",
    )(page_table, q_rows, bias_col, own, pick, jnp.transpose(pick), *([cache_k] * pps), *([cache_v] * pps),
      k_new, v_new)


def kernel(x_prompt, x_sample, c_prompt, c_sample, cache_k, cache_v, page_table, ada_w, ada_b, ln1_g, ln1_b, ln2_g, ln2_b, a_w_in, a_b_in, a_vnorm_g, a_vnorm_b, a_w_s, a_b_s, a_w_out, kv_w, b_w_q, b_w_o, b_logit_bias, moe_router_group_w, moe_router_group_b, moe_router_expert_w, moe_router_expert_b, moe_w_gate, moe_w_up, moe_w_down):
    bp, sp, d = x_prompt.shape
    bs, ss, _ = x_sample.shape
    depth = ada_w.shape[0]
    n_a = a_w_in.shape[0]
    d_a = a_w_out.shape[1]
    chunk = a_w_s.shape[-1]
    n_pool, page, n_heads, dh = cache_k.shape
    hd = n_heads * dh
    n_groups, n_per = moe_router_expert_b.shape[1:]
    n_experts = n_groups * n_per
    alpha = (2 * depth) ** 0.25
    assert n_groups + n_experts <= ROUTE_WIDTH and ss <= SUBLANES and sp % chunk == 0
    np_tok, ns_tok = bp * sp, bs * ss
    ts_p = min(512, sp)
    tiles_p = sp // ts_p
    assert np_tok % ns_tok == 0

    c_rows = -(-(bp + bs) // SUBLANES) * SUBLANES
    c_all = jnp.concatenate([c_prompt, c_sample, jnp.zeros((c_rows - bp - bs, d), F32)], axis=0)
    mod = _modulation(c_all, ada_w, ada_b)

    def mods(l, group):
        m = mod[l].reshape(c_rows, 6, d)
        if group == 0:
            return [m[:bp, j].reshape(bp, 1, d) for j in range(6)]
        return [jnp.repeat(m[bp:bp + bs, j], ss, axis=0).reshape(1, ns_tok, d) for j in range(6)]

    groups = [dict(x=x_prompt.reshape(np_tok, d), ts=ts_p, tiles=tiles_p, per_token=False, n=np_tok),
              dict(x=x_sample.reshape(ns_tok, d), ts=ns_tok, tiles=1, per_token=True, n=ns_tok)]
    tm = 128
    slot_stride = -(-(np_tok + ns_tok) // ts_p) * ts_p
    assert slot_stride % ns_tok == 0 and (n_heads * ss) % SUBLANES == 0
    chunk_v = []
    k_out, v_out = [None, None], [None, None]

    for l in range(depth):
        w_route = jnp.concatenate(
            [moe_router_group_w[l],
             jnp.transpose(moe_router_expert_w[l], (1, 0, 2)).reshape(d, n_experts),
             jnp.zeros((d, ROUTE_WIDTH - n_groups - n_experts), F32)], axis=1).astype(BF16)
        b_route = jnp.concatenate(
            [moe_router_group_b[l], moe_router_expert_b[l].reshape(-1),
             jnp.zeros((ROUTE_WIDTH - n_groups - n_experts,), F32)]).reshape(1, ROUTE_WIDTH)
        x1s, h2s, routes, g2s = [], [], [], []
        for gi, grp in enumerate(groups):
            sh1, sc1, g1, sh2, sc2, g2 = mods(l, gi)
            x, ts, tiles, per_token = grp["x"], grp["ts"], grp["tiles"], grp["per_token"]
            if l < n_a:
                u, v_raw = _projection(x, a_w_in[l].astype(BF16), (0, d_a), d_a, (F32, F32), ts=ts,
                                       tiles_per_seq=tiles, mod=(sc1, sh1, per_token),
                                       bias=a_b_in[l].reshape(1, 2 * d_a), act=True, name="gmlp_in")
                bs_t = jnp.transpose(a_b_s[l])
                if gi == 0:
                    n_chunks = np_tok // chunk
                    gated, = _spatial_gate(v_raw.reshape(n_chunks, chunk, d_a), u.reshape(n_chunks, chunk, d_a),
                                           a_vnorm_g[l].reshape(1, d_a), a_vnorm_b[l].reshape(1, d_a),
                                           a_w_s[l], bs_t, emit_vn=False)
                    a_mix = gated.reshape(np_tok, d_a)
                else:
                    pad = ((0, 0), (0, SAMPLE_ROWS - ss), (0, 0))
                    gated, vn = _spatial_gate(jnp.pad(v_raw.reshape(bs, ss, d_a), pad),
                                              jnp.pad(u.reshape(bs, ss, d_a), pad),
                                              a_vnorm_g[l].reshape(1, d_a), a_vnorm_b[l].reshape(1, d_a),
                                              a_w_s[l][:, :SAMPLE_ROWS, :SAMPLE_ROWS], bs_t[:SAMPLE_ROWS], emit_vn=True)
                    a_mix = gated[:, :ss].reshape(ns_tok, d_a)
                    chunk_v.append(vn[:, :ss])
                w_mix = a_w_out[l].astype(BF16)
            else:
                j = l - n_a
                if k_out[gi] is None:
                    k_out[gi], v_out[gi] = _projection(x, kv_w.astype(BF16), (0, hd), hd, (F32, F32), ts=ts,
                                                       tiles_per_seq=tiles, name="kv_proj")
                q, = _projection(x, b_w_q[j].astype(BF16), (0,), hd, (BF16,), ts=ts, tiles_per_seq=tiles,
                                 mod=(sc1, sh1, per_token), name="q_proj")
                if gi == 0:
                    o = _attention_prompt(q.reshape(bp, sp, hd), k_out[gi].reshape(bp, sp, hd),
                                          v_out[gi].reshape(bp, sp, hd), b_logit_bias[j], n_heads=n_heads)
                    a_mix = o.reshape(np_tok, hd)
                else:
                    q_rows = jnp.transpose(q.reshape(bs, ss, n_heads, dh), (0, 2, 1, 3)).reshape(bs, n_heads * ss, dh)

                    def new_page(t):
                        t = jnp.pad(t.reshape(bs, ss, n_heads, dh), ((0, 0), (0, page - ss), (0, 0), (0, 0)))
                        return t.reshape(bs, page * n_heads, dh)

                    o = _attention_paged(q_rows, jnp.repeat(b_logit_bias[j], ss).reshape(n_heads * ss, 1),
                                         cache_k.reshape(n_pool, page * n_heads, dh),
                                         cache_v.reshape(n_pool, page * n_heads, dh),
                                         new_page(k_out[gi]), new_page(v_out[gi]), page_table,
                                         n_heads=n_heads, nq=ss)
                    o = jnp.transpose(o.reshape(bs, n_heads, ss, dh), (0, 2, 1, 3))
                    a_mix = o.reshape(ns_tok, hd).astype(BF16)
                w_mix = b_w_o[j].astype(BF16)
            x1, h2, route = _out_projection(a_mix, w_mix, x, g1, ln1_g[l].reshape(1, d), ln1_b[l].reshape(1, d),
                                            sc2, sh2, w_route, b_route, per_token=per_token, ts=ts,
                                            tiles_per_seq=tiles, alpha=alpha, n_groups=n_groups, n_per=n_per)
            x1s.append(x1)
            h2s.append(h2)
            routes.append(route)
            g2s.append(g2)

        h2_all = jnp.concatenate(h2s, axis=0)
        be, nv, row_dst, row_gate = _dispatch(jnp.concatenate(routes, axis=0), n_experts, tm)
        out2 = _moe_experts(h2_all, be, nv, row_dst, row_gate, moe_w_gate, moe_w_up, moe_w_down, l, tm=tm,
                            slot_stride=slot_stride)
        first_rows = [0, np_tok]
        for gi, grp in enumerate(groups):
            grp["x"] = _combine(x1s[gi], out2, first_rows[gi], slot_stride, g2s[gi], ln2_g[l].reshape(1, d),
                                ln2_b[l].reshape(1, d), per_token=grp["per_token"], ts=grp["ts"],
                                tiles_per_seq=grp["tiles"], alpha=alpha)

    y_prompt = groups[0]["x"].reshape(bp, sp, d)
    y_sample = groups[1]["x"].reshape(bs, ss, d)
    return (y_prompt, y_sample,
            k_out[0].reshape(bp, sp, n_heads, dh), v_out[0].reshape(bp, sp, n_heads, dh),
            k_out[1].reshape(bs, ss, n_heads, dh), v_out[1].reshape(bs, ss, n_heads, dh),
            jnp.stack(chunk_v))
```

```python
import functools
import math

import jax
import jax.numpy as jnp
from jax import lax
from jax.experimental import pallas as pl
from jax.experimental.pallas import tpu as pltpu

F32 = jnp.float32
BF16 = jnp.bfloat16
LN_EPS = 1e-5
V7X_VMEM_LIMIT_BYTES = 56 * 1024 * 1024
LANES = 128
SUBLANES = 8
TOP_K = 2
ROUTE_WIDTH = LANES
EPILOGUE_ROWS = 128
DMA_UNROLL = 8
PAGES_PER_STEP = 4
SAMPLE_ROWS = 2 * SUBLANES


def _cparams(semantics):
    return pltpu.CompilerParams(dimension_semantics=semantics,
                                vmem_limit_bytes=V7X_VMEM_LIMIT_BYTES)


def _layer_norm(x, g, b):
    mu = jnp.mean(x, axis=-1, keepdims=True)
    xc = x - mu
    var = jnp.mean(xc * xc, axis=-1, keepdims=True)
    return xc * lax.rsqrt(var + LN_EPS) * g + b


def _gelu_tanh(x):
    c = math.sqrt(2.0 / math.pi)
    return x * (0.5 * (1.0 + jnp.tanh(c * (x + 0.044715 * (x * x * x)))))


def _log_sigmoid(z):
    return jnp.minimum(z, 0.0) - jnp.log(1.0 + jnp.exp(-jnp.abs(z)))


def _mod_spec(per_token, ts, tiles_per_seq, d):
    if per_token:
        return pl.BlockSpec((1, ts, d), lambda i, *_: (0, i, 0))
    return pl.BlockSpec((1, 1, d), lambda i, *_: (i // tiles_per_seq, 0, 0))


def _mod_kernel(c_ref, w_ref, b_ref, o_ref):
    c = c_ref[...].astype(BF16)
    w = w_ref[0].astype(BF16)
    o_ref[0] = jnp.dot(c, w, preferred_element_type=F32) + b_ref[0]


def _modulation(c_all, ada_w, ada_b):
    depth, d, d6 = ada_w.shape
    rows = c_all.shape[0]
    tn = 1024 if d6 % 1024 == 0 else d6
    return pl.pallas_call(
        _mod_kernel,
        grid=(depth, d6 // tn),
        in_specs=[pl.BlockSpec((rows, d), lambda l, n: (0, 0)),
                  pl.BlockSpec((1, d, tn), lambda l, n: (l, 0, n)),
                  pl.BlockSpec((1, 1, tn), lambda l, n: (l, 0, n))],
        out_specs=pl.BlockSpec((1, rows, tn), lambda l, n: (l, 0, n)),
        out_shape=jax.ShapeDtypeStruct((depth, rows, d6), F32),
        compiler_params=_cparams(("arbitrary", "arbitrary")),
        name="adaln_modulation",
    )(c_all, ada_w, ada_b.reshape(depth, 1, d6))


def _proj_kernel(*refs, n_out, modulate, has_bias, act, out_scale):
    refs = list(refs)
    x_ref = refs.pop(0)
    sc_ref = refs.pop(0) if modulate else None
    sh_ref = refs.pop(0) if modulate else None
    w_refs = [refs.pop(0) for _ in range(n_out)]
    b_refs = [refs.pop(0) for _ in range(n_out)] if has_bias else [None] * n_out
    o_refs = [refs.pop(0) for _ in range(n_out)]
    h_ref = refs.pop(0)

    @pl.when(pl.program_id(1) == 0)
    def _():
        x = x_ref[...]
        if modulate:
            x = x * (1.0 + sc_ref[0]) + sh_ref[0]
        h_ref[...] = x.astype(BF16)

    h = h_ref[...]
    for w_ref, b_ref, o_ref in zip(w_refs, b_refs, o_refs):
        z = jnp.dot(h, w_ref[...], preferred_element_type=F32)
        if has_bias:
            z = z + b_ref[...]
        if act:
            z = _gelu_tanh(z)
        if out_scale != 1.0:
            z = z * out_scale
        o_ref[...] = z.astype(o_ref.dtype)


def _projection(x, w, col_starts, width, out_dtypes, *, ts, tiles_per_seq, mod=None, bias=None,
                act=False, out_scale=1.0, name="projection"):
    n, d = x.shape
    tn = min(512, width)
    n_out = len(col_starts)
    modulate = mod is not None
    in_specs = [pl.BlockSpec((ts, d), lambda i, j: (i, 0))]
    args = [x]
    if modulate:
        sc, sh, per_token = mod
        in_specs += [_mod_spec(per_token, ts, tiles_per_seq, d)] * 2
        args += [sc, sh]
    for c0 in col_starts:
        in_specs.append(pl.BlockSpec((d, tn), functools.partial(lambda i, j, o: (0, j + o), o=c0 // tn)))
        args.append(w)
    if bias is not None:
        for c0 in col_starts:
            in_specs.append(pl.BlockSpec((1, tn), functools.partial(lambda i, j, o: (0, j + o), o=c0 // tn)))
            args.append(bias)
    return pl.pallas_call(
        functools.partial(_proj_kernel, n_out=n_out, modulate=modulate, has_bias=bias is not None, act=act,
                          out_scale=out_scale),
        grid=(n // ts, width // tn),
        in_specs=in_specs,
        out_specs=[pl.BlockSpec((ts, tn), lambda i, j: (i, j))] * n_out,
        out_shape=[jax.ShapeDtypeStruct((n, width), dt) for dt in out_dtypes],
        scratch_shapes=[pltpu.VMEM((ts, d), BF16)],
        compiler_params=_cparams(("parallel", "arbitrary")),
        name=name,
    )(*args)


def _gate_kernel(v_ref, u_ref, g_ref, b_ref, ws_ref, bs_ref, o_ref, *maybe_vn_ref, n_groups):
    v = v_ref[0]
    vn = _layer_norm(v, g_ref[...], b_ref[...])
    if maybe_vn_ref:
        maybe_vn_ref[0][0] = vn
    vb = vn.astype(BF16)
    rows, d_a = v.shape
    gd = d_a // n_groups
    causal = (lax.broadcasted_iota(jnp.int32, (rows, rows), 0)
              >= lax.broadcasted_iota(jnp.int32, (rows, rows), 1))
    for g in range(n_groups):
        w = jnp.where(causal, ws_ref[g], 0.0).astype(BF16)
        s = jnp.dot(w, vb[:, g * gd:(g + 1) * gd], preferred_element_type=F32) + bs_ref[:, g:g + 1]
        o_ref[0, :, g * gd:(g + 1) * gd] = (u_ref[0, :, g * gd:(g + 1) * gd].astype(F32) * s).astype(o_ref.dtype)


def _spatial_gate(v_raw, u, vn_g, vn_b, w_s, b_s_t, *, emit_vn):
    c, r, d_a = v_raw.shape
    n_groups = w_s.shape[0]
    blk = pl.BlockSpec((1, r, d_a), lambda i: (i, 0, 0))
    out_shape = [jax.ShapeDtypeStruct((c, r, d_a), BF16)]
    out_specs = [blk]
    if emit_vn:
        out_shape.append(jax.ShapeDtypeStruct((c, r, d_a), F32))
        out_specs.append(blk)
    return pl.pallas_call(
        functools.partial(_gate_kernel, n_groups=n_groups),
        grid=(c,),
        in_specs=[blk, blk,
                  pl.BlockSpec((1, d_a), lambda i: (0, 0)),
                  pl.BlockSpec((1, d_a), lambda i: (0, 0)),
                  pl.BlockSpec((n_groups, r, r), lambda i: (0, 0, 0)),
                  pl.BlockSpec((r, n_groups), lambda i: (0, 0))],
        out_specs=out_specs,
        out_shape=out_shape,
        compiler_params=_cparams(("parallel",)),
        name="spatial_gate",
    )(v_raw, u, vn_g, vn_b, w_s, b_s_t)


def _route(logits, n_groups, n_per):
    lane = lax.broadcasted_iota(jnp.int32, logits.shape, 1)
    lane_f = lane.astype(F32)
    neg = -jnp.inf
    big = float(2 * ROUTE_WIDTH)
    gl = jnp.where(lane < n_groups, logits, neg)
    gmax = jnp.max(gl, axis=-1, keepdims=True)
    g_sel = jnp.min(jnp.where(gl == gmax, lane_f, big), axis=-1, keepdims=True)
    g_weight = 1.0 / jnp.sum(jnp.exp(gl - gmax), axis=-1, keepdims=True)
    e_lane = lane_f - float(n_groups)
    in_group = (e_lane >= g_sel * n_per) & (e_lane < (g_sel + 1.0) * n_per)
    el = jnp.where(in_group, logits, neg)
    v1 = jnp.max(el, axis=-1, keepdims=True)
    i1 = jnp.min(jnp.where(el == v1, lane_f, big), axis=-1, keepdims=True)
    el2 = jnp.where(lane_f == i1, neg, el)
    v2 = jnp.max(el2, axis=-1, keepdims=True)
    i2 = jnp.min(jnp.where(el2 == v2, lane_f, big), axis=-1, keepdims=True)
    e2 = jnp.exp(v2 - v1)
    gate1 = g_weight * (1.0 / (1.0 + e2))
    gate2 = g_weight * (e2 / (1.0 + e2))
    id1 = i1 - float(n_groups)
    id2 = i2 - float(n_groups)
    return jnp.where(lane == 0, id1,
                     jnp.where(lane == 1, id2,
                               jnp.where(lane == 2, gate1,
                                         jnp.where(lane == 3, gate2, 0.0))))


def _outproj_kernel(a_ref, w_ref, x_ref, g1_ref, lng_ref, lnb_ref, sc2_ref, sh2_ref, wr_ref, br_ref,
                    x1_ref, h2_ref, rt_ref, acc_ref, *, alpha, n_groups, n_per):
    k = pl.program_id(1)
    nk = pl.num_programs(1)
    ts = acc_ref.shape[0]

    def epilogue(r):
        g1 = g1_ref[0] if g1_ref.shape[1] == 1 else g1_ref[0, r, :]
        sc2 = sc2_ref[0] if sc2_ref.shape[1] == 1 else sc2_ref[0, r, :]
        sh2 = sh2_ref[0] if sh2_ref.shape[1] == 1 else sh2_ref[0, r, :]
        t = alpha * x_ref[r, :] + (1.0 + g1) * acc_ref[r, :]
        x1 = _layer_norm(t, lng_ref[...], lnb_ref[...])
        x1_ref[r, :] = x1
        h2 = x1 * (1.0 + sc2) + sh2
        h2_ref[r, :] = h2
        logits = jnp.dot(h2.astype(BF16), wr_ref[...], preferred_element_type=F32) + br_ref[...]
        rt_ref[r, :] = _route(logits, n_groups, n_per)

    part = jnp.dot(a_ref[...], w_ref[...], preferred_element_type=F32)

    @pl.when(k == 0)
    def _():
        acc_ref[...] = part

    @pl.when(k > 0)
    def _():
        acc_ref[...] += part

    @pl.when(k == nk - 1)
    def _():
        rows = min(EPILOGUE_ROWS, ts)

        def slab(s, c):
            epilogue(pl.ds(pl.multiple_of(s * rows, rows), rows))
            return c

        lax.fori_loop(0, ts // rows, slab, 0)


def _out_projection(a, w, x, g1, ln_g, ln_b, sc2, sh2, w_route, b_route, *, per_token, ts, tiles_per_seq,
                    alpha, n_groups, n_per):
    n, kdim = a.shape
    d = w.shape[1]
    tk = min(512, kdim)
    mspec = _mod_spec(per_token, ts, tiles_per_seq, d)
    vec = pl.BlockSpec((1, d), lambda i, k: (0, 0))
    tile = pl.BlockSpec((ts, d), lambda i, k: (i, 0))
    return pl.pallas_call(
        functools.partial(_outproj_kernel, alpha=alpha, n_groups=n_groups, n_per=n_per),
        grid=(n // ts, kdim // tk),
        in_specs=[pl.BlockSpec((ts, tk), lambda i, k: (i, k)),
                  pl.BlockSpec((tk, d), lambda i, k: (k, 0)),
                  tile, mspec, vec, vec, mspec, mspec,
                  pl.BlockSpec((d, ROUTE_WIDTH), lambda i, k: (0, 0)),
                  pl.BlockSpec((1, ROUTE_WIDTH), lambda i, k: (0, 0))],
        out_specs=[tile, tile, pl.BlockSpec((ts, ROUTE_WIDTH), lambda i, k: (i, 0))],
        out_shape=[jax.ShapeDtypeStruct((n, d), F32), jax.ShapeDtypeStruct((n, d), F32),
                   jax.ShapeDtypeStruct((n, ROUTE_WIDTH), F32)],
        scratch_shapes=[pltpu.VMEM((ts, d), F32)],
        compiler_params=_cparams(("parallel", "arbitrary")),
        name="mixer_out_ln_route",
    )(a, w, x, g1, ln_g, ln_b, sc2, sh2, w_route, b_route)


def _moe_kernel(be_ref, nv_ref, rd_ref, h_hbm, wg_ref, wu_ref, wd_ref, gate_ref, out_hbm,
                xbuf, obuf, wgb, wub, wdb, gsem, ssem, *, tm, slot_stride, n_rows):
    i = pl.program_id(0)
    nb = pl.num_programs(0)
    slot = i % 2

    def gather_copy(blk, s, r):
        tok = rd_ref[blk * tm + r] >> 1
        return pltpu.make_async_copy(h_hbm.at[pl.ds(tok, 1)], xbuf.at[s, pl.ds(r, 1)], gsem.at[s])

    def scatter_copy(blk, s, r):
        a = rd_ref[blk * tm + r]
        dst = (a & 1) * slot_stride + (a >> 1)
        return pltpu.make_async_copy(obuf.at[s, pl.ds(r, 1)], out_hbm.at[pl.ds(dst, 1)], ssem.at[s])

    def for_rows(n, fn):
        n_groups = n // DMA_UNROLL

        def group(gidx, c):
            for u in range(DMA_UNROLL):
                fn(gidx * DMA_UNROLL + u)
            return c

        def tail(r, c):
            fn(r)
            return c

        lax.fori_loop(0, n_groups, group, 0)
        lax.fori_loop(n_groups * DMA_UNROLL, n, tail, 0)

    def wait_gather(s):
        pltpu.make_async_copy(h_hbm.at[pl.ds(0, tm)], xbuf.at[s], gsem.at[s]).wait()

    def wait_scatter(n, s):
        def copy(rows):
            return pltpu.make_async_copy(obuf.at[s, rows], out_hbm.at[rows], ssem.at[s])

        n_aligned = pl.multiple_of((n // SUBLANES) * SUBLANES, SUBLANES)

        @pl.when(n_aligned > 0)
        def _():
            copy(pl.ds(0, n_aligned)).wait()

        def tail(r, c):
            copy(pl.ds(0, 1)).wait()
            return c

        lax.fori_loop(n_aligned, n, tail, 0)

    used = nv_ref[i] > 0
    prev = jnp.maximum(i - 1, 0)

    @pl.when(i == 0)
    def _():
        obuf[...] = jnp.zeros_like(obuf)
        for half in range(TOP_K):
            for start in range(n_rows, slot_stride, tm):
                size = min(tm, slot_stride - start)
                pltpu.sync_copy(obuf.at[0, pl.ds(0, size)], out_hbm.at[pl.ds(half * slot_stride + start, size)])
        for_rows(tm, lambda r: gather_copy(0, 0, r).start())

    @pl.when(i >= 2)
    def _():
        wait_scatter(nv_ref[jnp.maximum(i - 2, 0)], slot)

    @pl.when(used)
    def _():
        wait_gather(slot)

        @pl.when((i == 0) | (be_ref[i] != be_ref[prev]))
        def _():
            wgb[...] = wg_ref[0, 0].astype(BF16)
            wub[...] = wu_ref[0, 0].astype(BF16)
            wdb[...] = wd_ref[0, 0].astype(BF16)

        for r in range(tm):
            gather_copy(i + 1, 1 - slot, r).start()
        x = xbuf[slot].astype(BF16)
        g = jnp.dot(x, wgb[...], preferred_element_type=F32)
        u = jnp.dot(x, wub[...], preferred_element_type=F32)
        mid = (g * jax.nn.sigmoid(g) * u).astype(BF16)
        o = jnp.dot(mid, wdb[...], preferred_element_type=F32)
        obuf[slot] = o * gate_ref[...]
        for_rows(nv_ref[i], lambda r: scatter_copy(i, slot, r).start())

    @pl.when(jnp.logical_not(used) & ((i == 0) | (nv_ref[prev] > 0)))
    def _():
        wait_gather(slot)

    @pl.when(i == nb - 1)
    def _():
        @pl.when(i >= 1)
        def _():
            wait_scatter(nv_ref[prev], 1 - slot)
        wait_scatter(nv_ref[i], slot)


def _moe_experts(h2, blk_expert, blk_nvalid, row_dst, row_gate, w_gate, w_up, w_down, layer, *, tm, slot_stride):
    n, d = h2.shape
    _, n_experts, _, de = w_gate.shape
    nb = blk_expert.shape[0] - 1
    grid_spec = pltpu.PrefetchScalarGridSpec(
        num_scalar_prefetch=3,
        grid=(nb,),
        in_specs=[pl.BlockSpec(memory_space=pl.ANY),
                  pl.BlockSpec((1, 1, d, de), lambda i, be, nv, rd: (layer, be[i], 0, 0)),
                  pl.BlockSpec((1, 1, d, de), lambda i, be, nv, rd: (layer, be[i], 0, 0)),
                  pl.BlockSpec((1, 1, de, d), lambda i, be, nv, rd: (layer, be[i], 0, 0)),
                  pl.BlockSpec((tm, 1), lambda i, be, nv, rd: (i, 0))],
        out_specs=pl.BlockSpec(memory_space=pl.ANY),
        scratch_shapes=[pltpu.VMEM((2, tm, d), F32), pltpu.VMEM((2, tm, d), F32),
                        pltpu.VMEM((d, de), BF16), pltpu.VMEM((d, de), BF16), pltpu.VMEM((de, d), BF16),
                        pltpu.SemaphoreType.DMA((2,)), pltpu.SemaphoreType.DMA((2,))],
    )
    return pl.pallas_call(
        functools.partial(_moe_kernel, tm=tm, slot_stride=slot_stride, n_rows=n),
        grid_spec=grid_spec,
        out_shape=jax.ShapeDtypeStruct((TOP_K * slot_stride, d), F32),
        compiler_params=_cparams(("arbitrary",)),
        name="moe_experts",
    )(blk_expert, blk_nvalid, row_dst, h2, w_gate, w_up, w_down, row_gate)


def _dispatch(route, n_experts, tm):
    n = route.shape[0]
    eid = route[:, :TOP_K].astype(jnp.int32).reshape(-1)
    gates = route[:, TOP_K:2 * TOP_K].reshape(-1)
    n_assign = n * TOP_K
    nb = -(-n_assign // tm) + n_experts + 2
    _, order, gates_sorted = lax.sort((eid, jnp.arange(n_assign, dtype=jnp.int32), gates), num_keys=1)
    experts = jnp.arange(n_experts, dtype=jnp.int32)
    starts = jnp.sum(eid[None, :] < experts[:, None], axis=1, dtype=jnp.int32)
    counts = jnp.sum(eid[None, :] == experts[:, None], axis=1, dtype=jnp.int32)
    nblk = (counts + tm - 1) // tm
    blk_end = jnp.cumsum(nblk)
    blk_start = blk_end - nblk
    blk = jnp.arange(nb, dtype=jnp.int32)
    n_used = blk_end[-1]
    be = jnp.minimum(jnp.sum(blk_end[None, :] <= blk[:, None], axis=1, dtype=jnp.int32), n_experts - 1)
    mine = be[:, None] == experts[None, :]

    def of_expert(per_expert):
        return jnp.sum(jnp.where(mine, per_expert[None, :], 0), axis=1, dtype=jnp.int32)

    used = blk < n_used
    first_row = (blk - of_expert(blk_start)) * tm
    nv = jnp.where(used, jnp.clip(of_expert(counts) - first_row, 0, tm), 0).astype(jnp.int32)
    r = jnp.arange(tm, dtype=jnp.int32)[None, :]
    src = jnp.clip((of_expert(starts) + first_row)[:, None] + r, 0, n_assign - 1)
    valid = r < nv[:, None]
    row_dst = jnp.where(valid, order[src], 0).reshape(nb * tm)
    row_gate = jnp.where(valid, gates_sorted[src], 0.0).reshape(nb * tm, 1)
    last_e = jnp.max(jnp.where(used, be, 0))
    be = jnp.where(used, be, last_e)
    return be, nv, row_dst, row_gate


def _combine_kernel(x1_ref, oa_ref, ob_ref, g2_ref, lng_ref, lnb_ref, x2_ref, *, alpha):
    y = oa_ref[...] + ob_ref[...]
    t = alpha * x1_ref[...] + (1.0 + g2_ref[0]) * y
    x2_ref[...] = _layer_norm(t, lng_ref[...], lnb_ref[...])


def _combine(x1, out2, first_row, slot_stride, g2, ln_g, ln_b, *, per_token, ts, tiles_per_seq, alpha):
    n, d = x1.shape
    vec = pl.BlockSpec((1, d), lambda i: (0, 0))
    tile = pl.BlockSpec((ts, d), lambda i: (i, 0))
    off_a = first_row // ts
    off_b = (first_row + slot_stride) // ts
    return pl.pallas_call(
        functools.partial(_combine_kernel, alpha=alpha),
        grid=(n // ts,),
        in_specs=[tile,
                  pl.BlockSpec((ts, d), lambda i: (i + off_a, 0)),
                  pl.BlockSpec((ts, d), lambda i: (i + off_b, 0)),
                  _mod_spec(per_token, ts, tiles_per_seq, d), vec, vec],
        out_specs=tile,
        out_shape=jax.ShapeDtypeStruct((n, d), F32),
        compiler_params=_cparams(("parallel",)),
        name="moe_combine_ln",
    )(x1, out2, out2, g2, ln_g, ln_b)


def _split2(x):
    hi = x.astype(BF16)
    return hi, (x - hi.astype(F32)).astype(BF16)


def _stick_terms(z, vis, excl_ones):
    lb = _log_sigmoid(z)
    ls = lb - z
    if vis is not None:
        ls = jnp.where(vis, ls, 0.0)
    hi, lo = _split2(ls)
    later = jnp.dot(hi, excl_ones, preferred_element_type=F32) + jnp.dot(lo, excl_ones, preferred_element_type=F32)
    return lb, later, jnp.sum(ls, axis=-1, keepdims=True)


def _stick_weights(lb, later, run, vis):
    a = jnp.exp(lb + (later + run))
    if vis is not None:
        a = jnp.where(vis, a, 0.0)
    return a


def _stick_block(z, vis, excl_ones, run):
    lb, later, total = _stick_terms(z, vis, excl_ones)
    return _stick_weights(lb, later, run, vis), run + total


def _excl_ones(tk):
    j = lax.broadcasted_iota(jnp.int32, (tk, tk), 0)
    s = lax.broadcasted_iota(jnp.int32, (tk, tk), 1)
    return jnp.where(j > s, 1.0, 0.0).astype(BF16)


def _attn_kernel(bias_ref, q_ref, k_ref, v_ref, o_ref, *, t, dh, heads):
    scale = dh ** -0.5
    hg = pl.program_id(1)
    qi = pl.program_id(2)
    ones = _excl_ones(t)
    below = (lax.broadcasted_iota(jnp.int32, (t, t), 1) < lax.broadcasted_iota(jnp.int32, (t, t), 0))

    def blocks(starts, carry, masks):
        out = []
        for hh in range(heads):
            acc, run = carry[hh]
            lanes = slice(hh * dh, (hh + 1) * dh)
            terms = []
            for start, vis in zip(starts, masks):
                k = k_ref[0, pl.ds(start, t), lanes].astype(BF16)
                z = lax.dot_general(q_ref[0, :, lanes], k, (((1,), (1,)), ((), ())),
                                    preferred_element_type=F32) * scale + bias_ref[hg * heads + hh]
                terms.append(_stick_terms(z, vis, ones))
            for start, vis, (lb, later, total) in zip(starts, masks, terms):
                a = _stick_weights(lb, later, run, vis)
                run = run + total
                v = v_ref[0, pl.ds(start, t), lanes].astype(BF16)
                acc = acc + jnp.dot(a.astype(BF16), v, preferred_element_type=F32)
            out.append((acc, run))
        return tuple(out)

    init = tuple((jnp.zeros((t, dh), F32), jnp.zeros((t, 1), F32)) for _ in range(heads))
    diag = pl.multiple_of(qi * t, t)
    carry = lax.cond(qi == 0,
                     lambda: blocks([diag], init, [below]),
                     lambda: blocks([diag, pl.multiple_of(diag - t, t)], init, [below, None]))
    rest = jnp.maximum(qi - 1, 0)

    def pair(j, c):
        newer = pl.multiple_of((rest - 1 - 2 * j) * t, t)
        return blocks([newer, pl.multiple_of(newer - t, t)], c, [None, None])

    carry = lax.fori_loop(0, rest // 2, pair, carry)
    carry = lax.fori_loop(0, rest % 2, lambda j, c: blocks([0], c, [None]), carry)
    for hh in range(heads):
        o_ref[0, :, hh * dh:(hh + 1) * dh] = carry[hh][0].astype(o_ref.dtype)


def _attention_prompt(q, k, v, logit_bias, *, n_heads):
    b, s, hd = q.shape
    dh = hd // n_heads
    t = min(256, s)
    heads = 2 if n_heads % 2 == 0 else 1
    w = heads * dh
    return pl.pallas_call(
        functools.partial(_attn_kernel, t=t, dh=dh, heads=heads),
        grid=(b, n_heads // heads, s // t),
        in_specs=[pl.BlockSpec(memory_space=pltpu.SMEM),
                  pl.BlockSpec((1, t, w), lambda bi, h, qi: (bi, qi, h)),
                  pl.BlockSpec((1, s, w), lambda bi, h, qi: (bi, 0, h)),
                  pl.BlockSpec((1, s, w), lambda bi, h, qi: (bi, 0, h))],
        out_specs=pl.BlockSpec((1, t, w), lambda bi, h, qi: (bi, qi, h)),
        out_shape=jax.ShapeDtypeStruct((b, s, hd), BF16),
        compiler_params=_cparams(("parallel", "parallel", "arbitrary")),
        name="stick_attention_prompt",
    )(logit_bias, q, k, v)


def _attn_paged_kernel(pt_ref, q_ref, bias_ref, own_ref, pick_ref, spread_ref, *rest, page, nq, pps):
    k_refs, v_refs = rest[:pps], rest[pps:2 * pps]
    kn_ref, vn_ref, o_ref, acc_ref, run_ref = rest[2 * pps:]
    step = pl.program_id(1)
    scale = q_ref.shape[-1] ** -0.5

    @pl.when(step == 0)
    def _():
        acc_ref[...] = jnp.zeros_like(acc_ref)
        run_ref[...] = jnp.zeros_like(run_ref)

    def process(pages, new_keys):
        own = own_ref[...]
        ones = _excl_ones(page)
        terms = []
        for k_ref, _ in pages:
            s_all = lax.dot_general(q_ref[0], k_ref[0].astype(BF16), (((1,), (1,)), ((), ())),
                                    preferred_element_type=F32)
            hi, lo = _split2(s_all * own)
            z = (jnp.dot(hi, pick_ref[...], preferred_element_type=F32)
                 + jnp.dot(lo, pick_ref[...], preferred_element_type=F32)) * scale + bias_ref[...]
            vis = None
            if new_keys:
                t = lax.broadcasted_iota(jnp.int32, z.shape, 0) % nq
                vis = lax.broadcasted_iota(jnp.int32, z.shape, 1) < t
            terms.append(_stick_terms(z, vis, ones) + (vis,))
        run = run_ref[...]
        acc = acc_ref[...]
        for (lb, later, total, vis), (_, v_ref) in zip(terms, pages):
            a = _stick_weights(lb, later, run, vis)
            run = run + total
            a_all = (jnp.dot(a.astype(BF16), spread_ref[...], preferred_element_type=F32) * own).astype(BF16)
            acc = acc + jnp.dot(a_all, v_ref[0].astype(BF16), preferred_element_type=F32)
        run_ref[...] = run
        acc_ref[...] = acc

    @pl.when(step == 0)
    def _():
        process([(kn_ref, vn_ref)], True)

    @pl.when(step > 0)
    def _():
        process(list(zip(k_refs, v_refs)), False)

    @pl.when(step == pl.num_programs(1) - 1)
    def _():
        o_ref[0] = acc_ref[...]


def _attention_paged(q_rows, bias_col, cache_k, cache_v, k_new, v_new, page_table, *, n_heads, nq):
    b, rows, dh = q_rows.shape
    n_pages = page_table.shape[1]
    cols = cache_k.shape[1]
    page = cols // n_heads
    col = jnp.arange(cols, dtype=jnp.int32)
    own = (col[None, :] % n_heads == jnp.arange(rows, dtype=jnp.int32)[:, None] // nq).astype(F32)
    pick = (col[:, None] // n_heads == jnp.arange(page, dtype=jnp.int32)[None, :]).astype(BF16)

    pps = next(p for p in (PAGES_PER_STEP, 2, 1) if n_pages % p == 0)

    def page_spec(j):
        return pl.BlockSpec(
            (1, cols, dh), lambda bi, s, pt: (pt[bi, n_pages - 1 - (jnp.maximum(s, 1) - 1) * pps - j], 0, 0))

    def const(shape):
        return pl.BlockSpec(shape, lambda bi, s, pt: (0,) * len(shape))

    grid_spec = pltpu.PrefetchScalarGridSpec(
        num_scalar_prefetch=1,
        grid=(b, n_pages // pps + 1),
        in_specs=[pl.BlockSpec((1, rows, dh), lambda bi, s, pt: (bi, 0, 0)),
                  const((rows, 1)), const((rows, cols)), const((cols, page)), const((page, cols))]
                 + [page_spec(j) for j in range(pps)] * 2
                 + [pl.BlockSpec((1, cols, dh), lambda bi, s, pt: (bi, 0, 0))] * 2,
        out_specs=pl.BlockSpec((1, rows, dh), lambda bi, s, pt: (bi, 0, 0)),
        scratch_shapes=[pltpu.VMEM((rows, dh), F32), pltpu.VMEM((rows, 1), F32)],
    )
    return pl.pallas_call(
        functools.partial(_attn_paged_kernel, page=page, nq=nq, pps=pps),
        grid_spec=grid_spec,
        out_shape=jax.ShapeDtypeStruct((b, rows, dh), F32),
        compiler_params=_cparams(("parallel", "arbitrary")),
        name="stick_attention_paged",
    )(page_table, q_rows, bias_col, own, pick, jnp.transpose(pick), *([cache_k] * pps), *([cache_v] * pps),
      k_new, v_new)


def kernel(x_prompt, x_sample, c_prompt, c_sample, cache_k, cache_v, page_table, ada_w, ada_b, ln1_g, ln1_b, ln2_g, ln2_b, a_w_in, a_b_in, a_vnorm_g, a_vnorm_b, a_w_s, a_b_s, a_w_out, kv_w, b_w_q, b_w_o, b_logit_bias, moe_router_group_w, moe_router_group_b, moe_router_expert_w, moe_router_expert_b, moe_w_gate, moe_w_up, moe_w_down):
    bp, sp, d = x_prompt.shape
    bs, ss, _ = x_sample.shape
    depth = ada_w.shape[0]
    n_a = a_w_in.shape[0]
    d_a = a_w_out.shape[1]
    chunk = a_w_s.shape[-1]
    n_pool, page, n_heads, dh = cache_k.shape
    hd = n_heads * dh
    n_groups, n_per = moe_router_expert_b.shape[1:]
    n_experts = n_groups * n_per
    alpha = (2 * depth) ** 0.25
    assert n_groups + n_experts <= ROUTE_WIDTH and ss <= SUBLANES and sp % chunk == 0
    np_tok, ns_tok = bp * sp, bs * ss
    ts_p = min(512, sp)
    tiles_p = sp // ts_p
    assert np_tok % ns_tok == 0

    c_rows = -(-(bp + bs) // SUBLANES) * SUBLANES
    c_all = jnp.concatenate([c_prompt, c_sample, jnp.zeros((c_rows - bp - bs, d), F32)], axis=0)
    mod = _modulation(c_all, ada_w, ada_b)

    def mods(l, group):
        m = mod[l].reshape(c_rows, 6, d)
        if group == 0:
            return [m[:bp, j].reshape(bp, 1, d) for j in range(6)]
        return [jnp.repeat(m[bp:bp + bs, j], ss, axis=0).reshape(1, ns_tok, d) for j in range(6)]

    groups = [dict(x=x_prompt.reshape(np_tok, d), ts=ts_p, tiles=tiles_p, per_token=False, n=np_tok),
              dict(x=x_sample.reshape(ns_tok, d), ts=ns_tok, tiles=1, per_token=True, n=ns_tok)]
    tm = 128
    slot_stride = -(-(np_tok + ns_tok) // ts_p) * ts_p
    assert slot_stride % ns_tok == 0 and (n_heads * ss) % SUBLANES == 0
    chunk_v = []
    k_out, v_out = [None, None], [None, None]

    for l in range(depth):
        w_route = jnp.concatenate(
            [moe_router_group_w[l],
             jnp.transpose(moe_router_expert_w[l], (1, 0, 2)).reshape(d, n_experts),
             jnp.zeros((d, ROUTE_WIDTH - n_groups - n_experts), F32)], axis=1).astype(BF16)
        b_route = jnp.concatenate(
            [moe_router_group_b[l], moe_router_expert_b[l].reshape(-1),
             jnp.zeros((ROUTE_WIDTH - n_groups - n_experts,), F32)]).reshape(1, ROUTE_WIDTH)
        x1s, h2s, routes, g2s = [], [], [], []
        for gi, grp in enumerate(groups):
            sh1, sc1, g1, sh2, sc2, g2 = mods(l, gi)
            x, ts, tiles, per_token = grp["x"], grp["ts"], grp["tiles"], grp["per_token"]
            if l < n_a:
                u, v_raw = _projection(x, a_w_in[l].astype(BF16), (0, d_a), d_a, (F32, F32), ts=ts,
                                       tiles_per_seq=tiles, mod=(sc1, sh1, per_token),
                                       bias=a_b_in[l].reshape(1, 2 * d_a), act=True, name="gmlp_in")
                bs_t = jnp.transpose(a_b_s[l])
                if gi == 0:
                    n_chunks = np_tok // chunk
                    gated, = _spatial_gate(v_raw.reshape(n_chunks, chunk, d_a), u.reshape(n_chunks, chunk, d_a),
                                           a_vnorm_g[l].reshape(1, d_a), a_vnorm_b[l].reshape(1, d_a),
                                           a_w_s[l], bs_t, emit_vn=False)
                    a_mix = gated.reshape(np_tok, d_a)
                else:
                    pad = ((0, 0), (0, SAMPLE_ROWS - ss), (0, 0))
                    gated, vn = _spatial_gate(jnp.pad(v_raw.reshape(bs, ss, d_a), pad),
                                              jnp.pad(u.reshape(bs, ss, d_a), pad),
                                              a_vnorm_g[l].reshape(1, d_a), a_vnorm_b[l].reshape(1, d_a),
                                              a_w_s[l][:, :SAMPLE_ROWS, :SAMPLE_ROWS], bs_t[:SAMPLE_ROWS], emit_vn=True)
                    a_mix = gated[:, :ss].reshape(ns_tok, d_a)
                    chunk_v.append(vn[:, :ss])
                w_mix = a_w_out[l].astype(BF16)
            else:
                j = l - n_a
                if k_out[gi] is None:
                    k_out[gi], v_out[gi] = _projection(x, kv_w.astype(BF16), (0, hd), hd, (F32, F32), ts=ts,
                                                       tiles_per_seq=tiles, name="kv_proj")
                q, = _projection(x, b_w_q[j].astype(BF16), (0,), hd, (BF16,), ts=ts, tiles_per_seq=tiles,
                                 mod=(sc1, sh1, per_token), name="q_proj")
                if gi == 0:
                    o = _attention_prompt(q.reshape(bp, sp, hd), k_out[gi].reshape(bp, sp, hd),
                                          v_out[gi].reshape(bp, sp, hd), b_logit_bias[j], n_heads=n_heads)
                    a_mix = o.reshape(np_tok, hd)
                else:
                    q_rows = jnp.transpose(q.reshape(bs, ss, n_heads, dh), (0, 2, 1, 3)).reshape(bs, n_heads * ss, dh)

                    def new_page(t):
                        t = jnp.pad(t.reshape(bs, ss, n_heads, dh), ((0, 0), (0, page - ss), (0, 0), (0, 0)))
                        return t.reshape(bs, page * n_heads, dh)

                    o = _attention_paged(q_rows, jnp.repeat(b_logit_bias[j], ss).reshape(n_heads * ss, 1),
                                         cache_k.reshape(n_pool, page * n_heads, dh),
                                         cache_v.reshape(n_pool, page * n_heads, dh),
                                         new_page(k_out[gi]), new_page(v_out[gi]), page_table,
                                         n_heads=n_heads, nq=ss)
                    o = jnp.transpose(o.reshape(bs, n_heads, ss, dh), (0, 2, 1, 3))
                    a_mix = o.reshape(ns_tok, hd).astype(BF16)
                w_mix = b_w_o[j].astype(BF16)
            x1, h2, route = _out_projection(a_mix, w_mix, x, g1, ln1_g[l].reshape(1, d), ln1_b[l].reshape(1, d),
                                            sc2, sh2, w_route, b_route, per_token=per_token, ts=ts,
                                            tiles_per_seq=tiles, alpha=alpha, n_groups=n_groups, n_per=n_per)
            x1s.append(x1)
            h2s.append(h2)
            routes.append(route)
            g2s.append(g2)

        h2_all = jnp.concatenate(h2s, axis=0)
        be, nv, row_dst, row_gate = _dispatch(jnp.concatenate(routes, axis=0), n_experts, tm)
        out2 = _moe_experts(h2_all, be, nv, row_dst, row_gate, moe_w_gate, moe_w_up, moe_w_down, l, tm=tm,
                            slot_stride=slot_stride)
        first_rows = [0, np_tok]
        for gi, grp in enumerate(groups):
            grp["x"] = _combine(x1s[gi], out2, first_rows[gi], slot_stride, g2s[gi], ln2_g[l].reshape(1, d),
                                ln2_b[l].reshape(1, d), per_token=grp["per_token"], ts=grp["ts"],
                                tiles_per_seq=grp["tiles"], alpha=alpha)

    y_prompt = groups[0]["x"].reshape(bp, sp, d)
    y_sample = groups[1]["x"].reshape(bs, ss, d)
    return (y_prompt, y_sample,
            k_out[0].reshape(bp, sp, n_heads, dh), v_out[0].reshape(bp, sp, n_heads, dh),
            k_out[1].reshape(bs, ss, n_heads, dh), v_out[1].reshape(bs, ss, n_heads, dh),
            jnp.stack(chunk_v))
```

```python
import functools
import math

import jax
import jax.numpy as jnp
from jax import lax
from jax.experimental import pallas as pl
from jax.experimental.pallas import tpu as pltpu

F32 = jnp.float32
BF16 = jnp.bfloat16
LN_EPS = 1e-5
V7X_VMEM_LIMIT_BYTES = 56 * 1024 * 1024
LANES = 128
SUBLANES = 8
TOP_K = 2
ROUTE_WIDTH = LANES
EPILOGUE_ROWS = 128
DMA_UNROLL = 8
PAGES_PER_STEP = 4
SAMPLE_ROWS = 2 * SUBLANES


def _cparams(semantics):
    return pltpu.CompilerParams(dimension_semantics=semantics,
                                vmem_limit_bytes=V7X_VMEM_LIMIT_BYTES)


def _layer_norm(x, g, b):
    mu = jnp.mean(x, axis=-1, keepdims=True)
    xc = x - mu
    var = jnp.mean(xc * xc, axis=-1, keepdims=True)
    return xc * lax.rsqrt(var + LN_EPS) * g + b


def _gelu_tanh(x):
    c = math.sqrt(2.0 / math.pi)
    return x * (0.5 * (1.0 + jnp.tanh(c * (x + 0.044715 * (x * x * x)))))


def _log_sigmoid(z):
    return jnp.minimum(z, 0.0) - jnp.log(1.0 + jnp.exp(-jnp.abs(z)))


def _mod_spec(per_token, ts, tiles_per_seq, d):
    if per_token:
        return pl.BlockSpec((1, ts, d), lambda i, *_: (0, i, 0))
    return pl.BlockSpec((1, 1, d), lambda i, *_: (i // tiles_per_seq, 0, 0))


def _mod_kernel(c_ref, w_ref, b_ref, o_ref):
    c = c_ref[...].astype(BF16)
    w = w_ref[0].astype(BF16)
    o_ref[0] = jnp.dot(c, w, preferred_element_type=F32) + b_ref[0]


def _modulation(c_all, ada_w, ada_b):
    depth, d, d6 = ada_w.shape
    rows = c_all.shape[0]
    tn = 1024 if d6 % 1024 == 0 else d6
    return pl.pallas_call(
        _mod_kernel,
        grid=(depth, d6 // tn),
        in_specs=[pl.BlockSpec((rows, d), lambda l, n: (0, 0)),
                  pl.BlockSpec((1, d, tn), lambda l, n: (l, 0, n)),
                  pl.BlockSpec((1, 1, tn), lambda l, n: (l, 0, n))],
        out_specs=pl.BlockSpec((1, rows, tn), lambda l, n: (l, 0, n)),
        out_shape=jax.ShapeDtypeStruct((depth, rows, d6), F32),
        compiler_params=_cparams(("arbitrary", "arbitrary")),
        name="adaln_modulation",
    )(c_all, ada_w, ada_b.reshape(depth, 1, d6))


def _proj_kernel(*refs, n_out, modulate, has_bias, act, out_scale):
    refs = list(refs)
    x_ref = refs.pop(0)
    sc_ref = refs.pop(0) if modulate else None
    sh_ref = refs.pop(0) if modulate else None
    w_refs = [refs.pop(0) for _ in range(n_out)]
    b_refs = [refs.pop(0) for _ in range(n_out)] if has_bias else [None] * n_out
    o_refs = [refs.pop(0) for _ in range(n_out)]
    h_ref = refs.pop(0)

    @pl.when(pl.program_id(1) == 0)
    def _():
        x = x_ref[...]
        if modulate:
            x = x * (1.0 + sc_ref[0]) + sh_ref[0]
        h_ref[...] = x.astype(BF16)

    h = h_ref[...]
    for w_ref, b_ref, o_ref in zip(w_refs, b_refs, o_refs):
        z = jnp.dot(h, w_ref[...], preferred_element_type=F32)
        if has_bias:
            z = z + b_ref[...]
        if act:
            z = _gelu_tanh(z)
        if out_scale != 1.0:
            z = z * out_scale
        o_ref[...] = z.astype(o_ref.dtype)


def _projection(x, w, col_starts, width, out_dtypes, *, ts, tiles_per_seq, mod=None, bias=None,
                act=False, out_scale=1.0, name="projection"):
    n, d = x.shape
    tn = min(512, width)
    n_out = len(col_starts)
    modulate = mod is not None
    in_specs = [pl.BlockSpec((ts, d), lambda i, j: (i, 0))]
    args = [x]
    if modulate:
        sc, sh, per_token = mod
        in_specs += [_mod_spec(per_token, ts, tiles_per_seq, d)] * 2
        args += [sc, sh]
    for c0 in col_starts:
        in_specs.append(pl.BlockSpec((d, tn), functools.partial(lambda i, j, o: (0, j + o), o=c0 // tn)))
        args.append(w)
    if bias is not None:
        for c0 in col_starts:
            in_specs.append(pl.BlockSpec((1, tn), functools.partial(lambda i, j, o: (0, j + o), o=c0 // tn)))
            args.append(bias)
    return pl.pallas_call(
        functools.partial(_proj_kernel, n_out=n_out, modulate=modulate, has_bias=bias is not None, act=act,
                          out_scale=out_scale),
        grid=(n // ts, width // tn),
        in_specs=in_specs,
        out_specs=[pl.BlockSpec((ts, tn), lambda i, j: (i, j))] * n_out,
        out_shape=[jax.ShapeDtypeStruct((n, width), dt) for dt in out_dtypes],
        scratch_shapes=[pltpu.VMEM((ts, d), BF16)],
        compiler_params=_cparams(("parallel", "arbitrary")),
        name=name,
    )(*args)


def _gate_kernel(v_ref, u_ref, g_ref, b_ref, ws_ref, bs_ref, o_ref, *maybe_vn_ref, n_groups):
    v = v_ref[0]
    vn = _layer_norm(v, g_ref[...], b_ref[...])
    if maybe_vn_ref:
        maybe_vn_ref[0][0] = vn
    vb = vn.astype(BF16)
    rows, d_a = v.shape
    gd = d_a // n_groups
    causal = (lax.broadcasted_iota(jnp.int32, (rows, rows), 0)
              >= lax.broadcasted_iota(jnp.int32, (rows, rows), 1))
    for g in range(n_groups):
        w = jnp.where(causal, ws_ref[g], 0.0).astype(BF16)
        s = jnp.dot(w, vb[:, g * gd:(g + 1) * gd], preferred_element_type=F32) + bs_ref[:, g:g + 1]
        o_ref[0, :, g * gd:(g + 1) * gd] = (u_ref[0, :, g * gd:(g + 1) * gd].astype(F32) * s).astype(o_ref.dtype)


def _spatial_gate(v_raw, u, vn_g, vn_b, w_s, b_s_t, *, emit_vn):
    c, r, d_a = v_raw.shape
    n_groups = w_s.shape[0]
    blk = pl.BlockSpec((1, r, d_a), lambda i: (i, 0, 0))
    out_shape = [jax.ShapeDtypeStruct((c, r, d_a), BF16)]
    out_specs = [blk]
    if emit_vn:
        out_shape.append(jax.ShapeDtypeStruct((c, r, d_a), F32))
        out_specs.append(blk)
    return pl.pallas_call(
        functools.partial(_gate_kernel, n_groups=n_groups),
        grid=(c,),
        in_specs=[blk, blk,
                  pl.BlockSpec((1, d_a), lambda i: (0, 0)),
                  pl.BlockSpec((1, d_a), lambda i: (0, 0)),
                  pl.BlockSpec((n_groups, r, r), lambda i: (0, 0, 0)),
                  pl.BlockSpec((r, n_groups), lambda i: (0, 0))],
        out_specs=out_specs,
        out_shape=out_shape,
        compiler_params=_cparams(("parallel",)),
        name="spatial_gate",
    )(v_raw, u, vn_g, vn_b, w_s, b_s_t)


def _route(logits, n_groups, n_per):
    lane = lax.broadcasted_iota(jnp.int32, logits.shape, 1)
    lane_f = lane.astype(F32)
    neg = -jnp.inf
    big = float(2 * ROUTE_WIDTH)
    gl = jnp.where(lane < n_groups, logits, neg)
    gmax = jnp.max(gl, axis=-1, keepdims=True)
    g_sel = jnp.min(jnp.where(gl == gmax, lane_f, big), axis=-1, keepdims=True)
    g_weight = 1.0 / jnp.sum(jnp.exp(gl - gmax), axis=-1, keepdims=True)
    e_lane = lane_f - float(n_groups)
    in_group = (e_lane >= g_sel * n_per) & (e_lane < (g_sel + 1.0) * n_per)
    el = jnp.where(in_group, logits, neg)
    v1 = jnp.max(el, axis=-1, keepdims=True)
    i1 = jnp.min(jnp.where(el == v1, lane_f, big), axis=-1, keepdims=True)
    el2 = jnp.where(lane_f == i1, neg, el)
    v2 = jnp.max(el2, axis=-1, keepdims=True)
    i2 = jnp.min(jnp.where(el2 == v2, lane_f, big), axis=-1, keepdims=True)
    e2 = jnp.exp(v2 - v1)
    gate1 = g_weight * (1.0 / (1.0 + e2))
    gate2 = g_weight * (e2 / (1.0 + e2))
    id1 = i1 - float(n_groups)
    id2 = i2 - float(n_groups)
    return jnp.where(lane == 0, id1,
                     jnp.where(lane == 1, id2,
                               jnp.where(lane == 2, gate1,
                                         jnp.where(lane == 3, gate2, 0.0))))


def _outproj_kernel(a_ref, w_ref, x_ref, g1_ref, lng_ref, lnb_ref, sc2_ref, sh2_ref, wr_ref, br_ref,
                    x1_ref, h2_ref, rt_ref, acc_ref, *, alpha, n_groups, n_per):
    k = pl.program_id(1)
    nk = pl.num_programs(1)
    ts = acc_ref.shape[0]

    def epilogue(r):
        g1 = g1_ref[0] if g1_ref.shape[1] == 1 else g1_ref[0, r, :]
        sc2 = sc2_ref[0] if sc2_ref.shape[1] == 1 else sc2_ref[0, r, :]
        sh2 = sh2_ref[0] if sh2_ref.shape[1] == 1 else sh2_ref[0, r, :]
        t = alpha * x_ref[r, :] + (1.0 + g1) * acc_ref[r, :]
        x1 = _layer_norm(t, lng_ref[...], lnb_ref[...])
        x1_ref[r, :] = x1
        h2 = x1 * (1.0 + sc2) + sh2
        h2_ref[r, :] = h2
        logits = jnp.dot(h2.astype(BF16), wr_ref[...], preferred_element_type=F32) + br_ref[...]
        rt_ref[r, :] = _route(logits, n_groups, n_per)

    @pl.when(k == 0)
    def _():
        acc_ref[...] = jnp.zeros_like(acc_ref)

    acc_ref[...] += jnp.dot(a_ref[...], w_ref[...], preferred_element_type=F32)

    @pl.when(k == nk - 1)
    def _():
        rows = min(EPILOGUE_ROWS, ts)

        def slab(s, c):
            epilogue(pl.ds(pl.multiple_of(s * rows, rows), rows))
            return c

        lax.fori_loop(0, ts // rows, slab, 0)


def _out_projection(a, w, x, g1, ln_g, ln_b, sc2, sh2, w_route, b_route, *, per_token, ts, tiles_per_seq,
                    alpha, n_groups, n_per):
    n, kdim = a.shape
    d = w.shape[1]
    tk = min(512, kdim)
    mspec = _mod_spec(per_token, ts, tiles_per_seq, d)
    vec = pl.BlockSpec((1, d), lambda i, k: (0, 0))
    tile = pl.BlockSpec((ts, d), lambda i, k: (i, 0))
    return pl.pallas_call(
        functools.partial(_outproj_kernel, alpha=alpha, n_groups=n_groups, n_per=n_per),
        grid=(n // ts, kdim // tk),
        in_specs=[pl.BlockSpec((ts, tk), lambda i, k: (i, k)),
                  pl.BlockSpec((tk, d), lambda i, k: (k, 0)),
                  tile, mspec, vec, vec, mspec, mspec,
                  pl.BlockSpec((d, ROUTE_WIDTH), lambda i, k: (0, 0)),
                  pl.BlockSpec((1, ROUTE_WIDTH), lambda i, k: (0, 0))],
        out_specs=[tile, tile, pl.BlockSpec((ts, ROUTE_WIDTH), lambda i, k: (i, 0))],
        out_shape=[jax.ShapeDtypeStruct((n, d), F32), jax.ShapeDtypeStruct((n, d), F32),
                   jax.ShapeDtypeStruct((n, ROUTE_WIDTH), F32)],
        scratch_shapes=[pltpu.VMEM((ts, d), F32)],
        compiler_params=_cparams(("parallel", "arbitrary")),
        name="mixer_out_ln_route",
    )(a, w, x, g1, ln_g, ln_b, sc2, sh2, w_route, b_route)


def _moe_kernel(be_ref, nv_ref, na_ref, rd_ref, ha_hbm, hb_hbm, wg_ref, wu_ref, wd_ref, gate_ref, out_hbm,
                xbuf, obuf, wgb, wub, wdb, gsem, ssem, *, tm, slot_stride, n_rows):
    i = pl.program_id(0)
    nb = pl.num_programs(0)
    slot = i % 2
    n_a = ha_hbm.shape[0]

    def gather_copy(blk, s, r, from_b):
        tok = rd_ref[blk * tm + r] >> 1
        src = hb_hbm.at[pl.ds(tok - n_a, 1)] if from_b else ha_hbm.at[pl.ds(tok, 1)]
        return pltpu.make_async_copy(src, xbuf.at[s, pl.ds(r, 1)], gsem.at[s])

    def scatter_copy(blk, s, r):
        a = rd_ref[blk * tm + r]
        dst = (a & 1) * slot_stride + (a >> 1)
        return pltpu.make_async_copy(obuf.at[s, pl.ds(r, 1)], out_hbm.at[pl.ds(dst, 1)], ssem.at[s])

    def for_rows(n, fn):
        n_groups = n // DMA_UNROLL

        def group(gidx, c):
            for u in range(DMA_UNROLL):
                fn(gidx * DMA_UNROLL + u)
            return c

        def tail(r, c):
            fn(r)
            return c

        lax.fori_loop(0, n_groups, group, 0)
        lax.fori_loop(n_groups * DMA_UNROLL, n, tail, 0)

    def start_gather(blk, s):
        n_from_a = na_ref[blk]
        for_rows(n_from_a, lambda r: gather_copy(blk, s, r, False).start())

        def from_b(r, c):
            gather_copy(blk, s, r, True).start()
            return c

        lax.fori_loop(n_from_a, nv_ref[blk], from_b, 0)

    def wait_rows(n, s, gather):
        def copy(rows):
            if gather:
                return pltpu.make_async_copy(ha_hbm.at[rows], xbuf.at[s, rows], gsem.at[s])
            return pltpu.make_async_copy(obuf.at[s, rows], out_hbm.at[rows], ssem.at[s])

        n_aligned = pl.multiple_of((n // SUBLANES) * SUBLANES, SUBLANES)

        @pl.when(n_aligned > 0)
        def _():
            copy(pl.ds(0, n_aligned)).wait()

        def tail(r, c):
            copy(pl.ds(0, 1)).wait()
            return c

        lax.fori_loop(n_aligned, n, tail, 0)

    prev = jnp.maximum(i - 1, 0)

    @pl.when(i == 0)
    def _():
        obuf[...] = jnp.zeros_like(obuf)
        for half in range(TOP_K):
            for start in range(n_rows, slot_stride, tm):
                size = min(tm, slot_stride - start)
                pltpu.sync_copy(obuf.at[0, pl.ds(0, size)], out_hbm.at[pl.ds(half * slot_stride + start, size)])
        xbuf[...] = jnp.zeros_like(xbuf)
        start_gather(0, 0)

    @pl.when(i + 1 < nb)
    def _():
        start_gather(jnp.minimum(i + 1, nb - 1), 1 - slot)

    @pl.when(i >= 2)
    def _():
        wait_rows(nv_ref[jnp.maximum(i - 2, 0)], slot, False)

    @pl.when(nv_ref[i] > 0)
    def _():
        wait_rows(nv_ref[i], slot, True)

        @pl.when((i == 0) | (be_ref[i] != be_ref[prev]))
        def _():
            wgb[...] = wg_ref[0, 0].astype(BF16)
            wub[...] = wu_ref[0, 0].astype(BF16)
            wdb[...] = wd_ref[0, 0].astype(BF16)

        x = xbuf[slot].astype(BF16)
        g = jnp.dot(x, wgb[...], preferred_element_type=F32)
        u = jnp.dot(x, wub[...], preferred_element_type=F32)
        mid = (g * jax.nn.sigmoid(g) * u).astype(BF16)
        o = jnp.dot(mid, wdb[...], preferred_element_type=F32)
        obuf[slot] = o * gate_ref[...]
        for_rows(nv_ref[i], lambda r: scatter_copy(i, slot, r).start())

    @pl.when(i == nb - 1)
    def _():
        @pl.when(i >= 1)
        def _():
            wait_rows(nv_ref[prev], 1 - slot, False)
        wait_rows(nv_ref[i], slot, False)


def _moe_experts(h2_a, h2_b, blk_expert, blk_nvalid, blk_from_a, row_dst, row_gate, w_gate, w_up, w_down, layer,
                 *, tm, slot_stride):
    d = h2_a.shape[1]
    n = h2_a.shape[0] + h2_b.shape[0]
    _, n_experts, _, de = w_gate.shape
    nb = blk_expert.shape[0]
    grid_spec = pltpu.PrefetchScalarGridSpec(
        num_scalar_prefetch=4,
        grid=(nb,),
        in_specs=[pl.BlockSpec(memory_space=pl.ANY), pl.BlockSpec(memory_space=pl.ANY),
                  pl.BlockSpec((1, 1, d, de), lambda i, be, nv, na, rd: (layer, be[i], 0, 0)),
                  pl.BlockSpec((1, 1, d, de), lambda i, be, nv, na, rd: (layer, be[i], 0, 0)),
                  pl.BlockSpec((1, 1, de, d), lambda i, be, nv, na, rd: (layer, be[i], 0, 0)),
                  pl.BlockSpec((tm, 1), lambda i, be, nv, na, rd: (i, 0))],
        out_specs=pl.BlockSpec(memory_space=pl.ANY),
        scratch_shapes=[pltpu.VMEM((2, tm, d), F32), pltpu.VMEM((2, tm, d), F32),
                        pltpu.VMEM((d, de), BF16), pltpu.VMEM((d, de), BF16), pltpu.VMEM((de, d), BF16),
                        pltpu.SemaphoreType.DMA((2,)), pltpu.SemaphoreType.DMA((2,))],
    )
    return pl.pallas_call(
        functools.partial(_moe_kernel, tm=tm, slot_stride=slot_stride, n_rows=n),
        grid_spec=grid_spec,
        out_shape=jax.ShapeDtypeStruct((TOP_K * slot_stride, d), F32),
        compiler_params=_cparams(("arbitrary",)),
        name="moe_experts",
    )(blk_expert, blk_nvalid, blk_from_a, row_dst, h2_a, h2_b, w_gate, w_up, w_down, row_gate)


def _dispatch(route, n_experts, tm, n_first):
    n = route.shape[0]
    eid = route[:, :TOP_K].astype(jnp.int32).reshape(-1)
    gates = route[:, TOP_K:2 * TOP_K].reshape(-1)
    n_assign = n * TOP_K
    nb = -(-n_assign // tm) + n_experts
    _, order, gates_sorted = lax.sort((eid, jnp.arange(n_assign, dtype=jnp.int32), gates), num_keys=1)
    experts = jnp.arange(n_experts, dtype=jnp.int32)
    starts = jnp.sum(eid[None, :] < experts[:, None], axis=1, dtype=jnp.int32)
    counts = jnp.sum(eid[None, :] == experts[:, None], axis=1, dtype=jnp.int32)
    nblk = (counts + tm - 1) // tm
    blk_end = jnp.cumsum(nblk)
    blk_start = blk_end - nblk
    blk = jnp.arange(nb, dtype=jnp.int32)
    n_used = blk_end[-1]
    be = jnp.minimum(jnp.sum(blk_end[None, :] <= blk[:, None], axis=1, dtype=jnp.int32), n_experts - 1)
    mine = be[:, None] == experts[None, :]

    def of_expert(per_expert):
        return jnp.sum(jnp.where(mine, per_expert[None, :], 0), axis=1, dtype=jnp.int32)

    used = blk < n_used
    first_row = (blk - of_expert(blk_start)) * tm
    nv = jnp.where(used, jnp.clip(of_expert(counts) - first_row, 0, tm), 0).astype(jnp.int32)
    r = jnp.arange(tm, dtype=jnp.int32)[None, :]
    src = jnp.clip((of_expert(starts) + first_row)[:, None] + r, 0, n_assign - 1)
    valid = r < nv[:, None]
    picked = order[src]
    n_from_first = jnp.sum(valid & (picked < n_first * TOP_K), axis=1, dtype=jnp.int32)
    row_dst = jnp.where(valid, picked, 0).reshape(nb * tm)
    row_gate = jnp.where(valid, gates_sorted[src], 0.0).reshape(nb * tm, 1)
    last_e = jnp.max(jnp.where(used, be, 0))
    be = jnp.where(used, be, last_e)
    return be, nv, n_from_first, row_dst, row_gate


def _combine_kernel(x1_ref, oa_ref, ob_ref, g2_ref, lng_ref, lnb_ref, x2_ref, *, alpha):
    y = oa_ref[...] + ob_ref[...]
    t = alpha * x1_ref[...] + (1.0 + g2_ref[0]) * y
    x2_ref[...] = _layer_norm(t, lng_ref[...], lnb_ref[...])


def _combine(x1, out2, first_row, slot_stride, g2, ln_g, ln_b, *, per_token, ts, tiles_per_seq, alpha):
    n, d = x1.shape
    vec = pl.BlockSpec((1, d), lambda i: (0, 0))
    tile = pl.BlockSpec((ts, d), lambda i: (i, 0))
    off_a = first_row // ts
    off_b = (first_row + slot_stride) // ts
    return pl.pallas_call(
        functools.partial(_combine_kernel, alpha=alpha),
        grid=(n // ts,),
        in_specs=[tile,
                  pl.BlockSpec((ts, d), lambda i: (i + off_a, 0)),
                  pl.BlockSpec((ts, d), lambda i: (i + off_b, 0)),
                  _mod_spec(per_token, ts, tiles_per_seq, d), vec, vec],
        out_specs=tile,
        out_shape=jax.ShapeDtypeStruct((n, d), F32),
        compiler_params=_cparams(("parallel",)),
        name="moe_combine_ln",
    )(x1, out2, out2, g2, ln_g, ln_b)


def _split2(x):
    hi = x.astype(BF16)
    return hi, (x - hi.astype(F32)).astype(BF16)


def _stick_terms(z, vis, excl_ones):
    lb = _log_sigmoid(z)
    ls = lb - z
    if vis is not None:
        ls = jnp.where(vis, ls, 0.0)
    hi, lo = _split2(ls)
    later = jnp.dot(hi, excl_ones, preferred_element_type=F32) + jnp.dot(lo, excl_ones, preferred_element_type=F32)
    return lb, later, jnp.sum(ls, axis=-1, keepdims=True)


def _stick_weights(lb, later, run, vis):
    a = jnp.exp(lb + (later + run))
    if vis is not None:
        a = jnp.where(vis, a, 0.0)
    return a


def _stick_block(z, vis, excl_ones, run):
    lb, later, total = _stick_terms(z, vis, excl_ones)
    return _stick_weights(lb, later, run, vis), run + total


def _excl_ones(tk):
    j = lax.broadcasted_iota(jnp.int32, (tk, tk), 0)
    s = lax.broadcasted_iota(jnp.int32, (tk, tk), 1)
    return jnp.where(j > s, 1.0, 0.0).astype(BF16)


def _attn_kernel(bias_ref, q_ref, k_ref, v_ref, o_ref, *, t, dh, heads):
    scale = dh ** -0.5
    hg = pl.program_id(1)
    qi = pl.program_id(2)
    ones = _excl_ones(t)
    below = (lax.broadcasted_iota(jnp.int32, (t, t), 1) < lax.broadcasted_iota(jnp.int32, (t, t), 0))

    def blocks(starts, carry, masks):
        out = []
        for hh in range(heads):
            acc, run = carry[hh]
            lanes = slice(hh * dh, (hh + 1) * dh)
            terms = []
            for start, vis in zip(starts, masks):
                k = k_ref[0, pl.ds(start, t), lanes].astype(BF16)
                z = lax.dot_general(q_ref[0, :, lanes], k, (((1,), (1,)), ((), ())),
                                    preferred_element_type=F32) * scale + bias_ref[hg * heads + hh]
                terms.append(_stick_terms(z, vis, ones))
            for start, vis, (lb, later, total) in zip(starts, masks, terms):
                a = _stick_weights(lb, later, run, vis)
                run = run + total
                v = v_ref[0, pl.ds(start, t), lanes].astype(BF16)
                acc = acc + jnp.dot(a.astype(BF16), v, preferred_element_type=F32)
            out.append((acc, run))
        return tuple(out)

    init = tuple((jnp.zeros((t, dh), F32), jnp.zeros((t, 1), F32)) for _ in range(heads))
    diag = pl.multiple_of(qi * t, t)
    carry = lax.cond(qi == 0,
                     lambda: blocks([diag], init, [below]),
                     lambda: blocks([diag, pl.multiple_of(diag - t, t)], init, [below, None]))
    rest = jnp.maximum(qi - 1, 0)

    def pair(j, c):
        newer = pl.multiple_of((rest - 1 - 2 * j) * t, t)
        return blocks([newer, pl.multiple_of(newer - t, t)], c, [None, None])

    carry = lax.fori_loop(0, rest // 2, pair, carry)
    carry = lax.fori_loop(0, rest % 2, lambda j, c: blocks([0], c, [None]), carry)
    for hh in range(heads):
        o_ref[0, :, hh * dh:(hh + 1) * dh] = carry[hh][0].astype(o_ref.dtype)


def _attention_prompt(q, k, v, logit_bias, *, n_heads):
    b, s, hd = q.shape
    dh = hd // n_heads
    t = min(256, s)
    heads = 2 if n_heads % 2 == 0 else 1
    w = heads * dh
    return pl.pallas_call(
        functools.partial(_attn_kernel, t=t, dh=dh, heads=heads),
        grid=(b, n_heads // heads, s // t),
        in_specs=[pl.BlockSpec(memory_space=pltpu.SMEM),
                  pl.BlockSpec((1, t, w), lambda bi, h, qi: (bi, qi, h)),
                  pl.BlockSpec((1, s, w), lambda bi, h, qi: (bi, 0, h)),
                  pl.BlockSpec((1, s, w), lambda bi, h, qi: (bi, 0, h))],
        out_specs=pl.BlockSpec((1, t, w), lambda bi, h, qi: (bi, qi, h)),
        out_shape=jax.ShapeDtypeStruct((b, s, hd), BF16),
        compiler_params=_cparams(("parallel", "parallel", "arbitrary")),
        name="stick_attention_prompt",
    )(logit_bias, q, k, v)


def _attn_paged_kernel(pt_ref, q_ref, bias_ref, own_ref, pick_ref, spread_ref, *rest, page, nq, pps):
    k_refs, v_refs = rest[:pps], rest[pps:2 * pps]
    kn_ref, vn_ref, o_ref, acc_ref, run_ref = rest[2 * pps:]
    step = pl.program_id(1)
    scale = q_ref.shape[-1] ** -0.5

    @pl.when(step == 0)
    def _():
        acc_ref[...] = jnp.zeros_like(acc_ref)
        run_ref[...] = jnp.zeros_like(run_ref)

    def process(pages, new_keys):
        own = own_ref[...]
        ones = _excl_ones(page)
        terms = []
        for k_ref, _ in pages:
            s_all = lax.dot_general(q_ref[0], k_ref[0].astype(BF16), (((1,), (1,)), ((), ())),
                                    preferred_element_type=F32)
            hi, lo = _split2(s_all * own)
            z = (jnp.dot(hi, pick_ref[...], preferred_element_type=F32)
                 + jnp.dot(lo, pick_ref[...], preferred_element_type=F32)) * scale + bias_ref[...]
            vis = None
            if new_keys:
                t = lax.broadcasted_iota(jnp.int32, z.shape, 0) % nq
                vis = lax.broadcasted_iota(jnp.int32, z.shape, 1) < t
            terms.append(_stick_terms(z, vis, ones) + (vis,))
        run = run_ref[...]
        acc = acc_ref[...]
        for (lb, later, total, vis), (_, v_ref) in zip(terms, pages):
            a = _stick_weights(lb, later, run, vis)
            run = run + total
            a_all = (jnp.dot(a.astype(BF16), spread_ref[...], preferred_element_type=F32) * own).astype(BF16)
            acc = acc + jnp.dot(a_all, v_ref[0].astype(BF16), preferred_element_type=F32)
        run_ref[...] = run
        acc_ref[...] = acc

    @pl.when(step == 0)
    def _():
        process([(kn_ref, vn_ref)], True)

    @pl.when(step > 0)
    def _():
        process(list(zip(k_refs, v_refs)), False)

    @pl.when(step == pl.num_programs(1) - 1)
    def _():
        o_ref[0] = acc_ref[...]


def _attention_paged(q_rows, bias_col, cache_k, cache_v, k_new, v_new, page_table, *, n_heads, nq):
    b, rows, dh = q_rows.shape
    n_pages = page_table.shape[1]
    cols = cache_k.shape[1]
    page = cols // n_heads
    col = jnp.arange(cols, dtype=jnp.int32)
    own = (col[None, :] % n_heads == jnp.arange(rows, dtype=jnp.int32)[:, None] // nq).astype(F32)
    pick = (col[:, None] // n_heads == jnp.arange(page, dtype=jnp.int32)[None, :]).astype(BF16)

    pps = next(p for p in (PAGES_PER_STEP, 2, 1) if n_pages % p == 0)

    def page_spec(j):
        return pl.BlockSpec(
            (1, cols, dh), lambda bi, s, pt: (pt[bi, n_pages - 1 - (jnp.maximum(s, 1) - 1) * pps - j], 0, 0))

    def const(shape):
        return pl.BlockSpec(shape, lambda bi, s, pt: (0,) * len(shape))

    grid_spec = pltpu.PrefetchScalarGridSpec(
        num_scalar_prefetch=1,
        grid=(b, n_pages // pps + 1),
        in_specs=[pl.BlockSpec((1, rows, dh), lambda bi, s, pt: (bi, 0, 0)),
                  const((rows, 1)), const((rows, cols)), const((cols, page)), const((page, cols))]
                 + [page_spec(j) for j in range(pps)] * 2
                 + [pl.BlockSpec((1, cols, dh), lambda bi, s, pt: (bi, 0, 0))] * 2,
        out_specs=pl.BlockSpec((1, rows, dh), lambda bi, s, pt: (bi, 0, 0)),
        scratch_shapes=[pltpu.VMEM((rows, dh), F32), pltpu.VMEM((rows, 1), F32)],
    )
    return pl.pallas_call(
        functools.partial(_attn_paged_kernel, page=page, nq=nq, pps=pps),
        grid_spec=grid_spec,
        out_shape=jax.ShapeDtypeStruct((b, rows, dh), F32),
        compiler_params=_cparams(("parallel", "arbitrary")),
        name="stick_attention_paged",
    )(page_table, q_rows, bias_col, own, pick, jnp.transpose(pick), *([cache_k] * pps), *([cache_v] * pps),
      k_new, v_new)


def kernel(x_prompt, x_sample, c_prompt, c_sample, cache_k, cache_v, page_table, ada_w, ada_b, ln1_g, ln1_b, ln2_g, ln2_b, a_w_in, a_b_in, a_vnorm_g, a_vnorm_b, a_w_s, a_b_s, a_w_out, kv_w, b_w_q, b_w_o, b_logit_bias, moe_router_group_w, moe_router_group_b, moe_router_expert_w, moe_router_expert_b, moe_w_gate, moe_w_up, moe_w_down):
    bp, sp, d = x_prompt.shape
    bs, ss, _ = x_sample.shape
    depth = ada_w.shape[0]
    n_a = a_w_in.shape[0]
    d_a = a_w_out.shape[1]
    chunk = a_w_s.shape[-1]
    n_pool, page, n_heads, dh = cache_k.shape
    hd = n_heads * dh
    n_groups, n_per = moe_router_expert_b.shape[1:]
    n_experts = n_groups * n_per
    alpha = (2 * depth) ** 0.25
    assert n_groups + n_experts <= ROUTE_WIDTH and ss <= SUBLANES and sp % chunk == 0
    np_tok, ns_tok = bp * sp, bs * ss
    ts_p = min(512, sp)
    tiles_p = sp // ts_p
    assert np_tok % ns_tok == 0

    c_rows = -(-(bp + bs) // SUBLANES) * SUBLANES
    c_all = jnp.concatenate([c_prompt, c_sample, jnp.zeros((c_rows - bp - bs, d), F32)], axis=0)
    mod = _modulation(c_all, ada_w, ada_b)

    def mods(l, group):
        m = mod[l].reshape(c_rows, 6, d)
        if group == 0:
            return [m[:bp, j].reshape(bp, 1, d) for j in range(6)]
        return [jnp.repeat(m[bp:bp + bs, j], ss, axis=0).reshape(1, ns_tok, d) for j in range(6)]

    groups = [dict(x=x_prompt.reshape(np_tok, d), ts=ts_p, tiles=tiles_p, per_token=False, n=np_tok),
              dict(x=x_sample.reshape(ns_tok, d), ts=ns_tok, tiles=1, per_token=True, n=ns_tok)]
    tm = 128
    slot_stride = -(-(np_tok + ns_tok) // ts_p) * ts_p
    assert slot_stride % ns_tok == 0 and (n_heads * ss) % SUBLANES == 0
    chunk_v = []
    k_out, v_out = [None, None], [None, None]

    for l in range(depth):
        w_route = jnp.concatenate(
            [moe_router_group_w[l],
             jnp.transpose(moe_router_expert_w[l], (1, 0, 2)).reshape(d, n_experts),
             jnp.zeros((d, ROUTE_WIDTH - n_groups - n_experts), F32)], axis=1).astype(BF16)
        b_route = jnp.concatenate(
            [moe_router_group_b[l], moe_router_expert_b[l].reshape(-1),
             jnp.zeros((ROUTE_WIDTH - n_groups - n_experts,), F32)]).reshape(1, ROUTE_WIDTH)
        x1s, h2s, routes, g2s = [], [], [], []
        for gi, grp in enumerate(groups):
            sh1, sc1, g1, sh2, sc2, g2 = mods(l, gi)
            x, ts, tiles, per_token = grp["x"], grp["ts"], grp["tiles"], grp["per_token"]
            if l < n_a:
                u, v_raw = _projection(x, a_w_in[l].astype(BF16), (0, d_a), d_a, (F32, F32), ts=ts,
                                       tiles_per_seq=tiles, mod=(sc1, sh1, per_token),
                                       bias=a_b_in[l].reshape(1, 2 * d_a), act=True, name="gmlp_in")
                bs_t = jnp.transpose(a_b_s[l])
                if gi == 0:
                    n_chunks = np_tok // chunk
                    gated, = _spatial_gate(v_raw.reshape(n_chunks, chunk, d_a), u.reshape(n_chunks, chunk, d_a),
                                           a_vnorm_g[l].reshape(1, d_a), a_vnorm_b[l].reshape(1, d_a),
                                           a_w_s[l], bs_t, emit_vn=False)
                    a_mix = gated.reshape(np_tok, d_a)
                else:
                    pad = ((0, 0), (0, SAMPLE_ROWS - ss), (0, 0))
                    gated, vn = _spatial_gate(jnp.pad(v_raw.reshape(bs, ss, d_a), pad),
                                              jnp.pad(u.reshape(bs, ss, d_a), pad),
                                              a_vnorm_g[l].reshape(1, d_a), a_vnorm_b[l].reshape(1, d_a),
                                              a_w_s[l][:, :SAMPLE_ROWS, :SAMPLE_ROWS], bs_t[:SAMPLE_ROWS], emit_vn=True)
                    a_mix = gated[:, :ss].reshape(ns_tok, d_a)
                    chunk_v.append(vn[:, :ss])
                w_mix = a_w_out[l].astype(BF16)
            else:
                j = l - n_a
                if k_out[gi] is None:
                    k_out[gi], v_out[gi] = _projection(x, kv_w.astype(BF16), (0, hd), hd, (F32, F32), ts=ts,
                                                       tiles_per_seq=tiles, name="kv_proj")
                q, = _projection(x, b_w_q[j].astype(BF16), (0,), hd, (BF16,), ts=ts, tiles_per_seq=tiles,
                                 mod=(sc1, sh1, per_token), name="q_proj")
                if gi == 0:
                    o = _attention_prompt(q.reshape(bp, sp, hd), k_out[gi].reshape(bp, sp, hd),
                                          v_out[gi].reshape(bp, sp, hd), b_logit_bias[j], n_heads=n_heads)
                    a_mix = o.reshape(np_tok, hd)
                else:
                    q_rows = jnp.transpose(q.reshape(bs, ss, n_heads, dh), (0, 2, 1, 3)).reshape(bs, n_heads * ss, dh)

                    def new_page(t):
                        t = jnp.pad(t.reshape(bs, ss, n_heads, dh), ((0, 0), (0, page - ss), (0, 0), (0, 0)))
                        return t.reshape(bs, page * n_heads, dh)

                    o = _attention_paged(q_rows, jnp.repeat(b_logit_bias[j], ss).reshape(n_heads * ss, 1),
                                         cache_k.reshape(n_pool, page * n_heads, dh),
                                         cache_v.reshape(n_pool, page * n_heads, dh),
                                         new_page(k_out[gi]), new_page(v_out[gi]), page_table,
                                         n_heads=n_heads, nq=ss)
                    o = jnp.transpose(o.reshape(bs, n_heads, ss, dh), (0, 2, 1, 3))
                    a_mix = o.reshape(ns_tok, hd).astype(BF16)
                w_mix = b_w_o[j].astype(BF16)
            x1, h2, route = _out_projection(a_mix, w_mix, x, g1, ln1_g[l].reshape(1, d), ln1_b[l].reshape(1, d),
                                            sc2, sh2, w_route, b_route, per_token=per_token, ts=ts,
                                            tiles_per_seq=tiles, alpha=alpha, n_groups=n_groups, n_per=n_per)
            x1s.append(x1)
            h2s.append(h2)
            routes.append(route)
            g2s.append(g2)

        be, nv, n_prompt_rows, row_dst, row_gate = _dispatch(jnp.concatenate(routes, axis=0), n_experts, tm, np_tok)
        out2 = _moe_experts(h2s[0], h2s[1], be, nv, n_prompt_rows, row_dst, row_gate, moe_w_gate, moe_w_up,
                            moe_w_down, l, tm=tm, slot_stride=slot_stride)
        first_rows = [0, np_tok]
        for gi, grp in enumerate(groups):
            grp["x"] = _combine(x1s[gi], out2, first_rows[gi], slot_stride, g2s[gi], ln2_g[l].reshape(1, d),
                                ln2_b[l].reshape(1, d), per_token=grp["per_token"], ts=grp["ts"],
                                tiles_per_seq=grp["tiles"], alpha=alpha)

    y_prompt = groups[0]["x"].reshape(bp, sp, d)
    y_sample = groups[1]["x"].reshape(bs, ss, d)
    return (y_prompt, y_sample,
            k_out[0].reshape(bp, sp, n_heads, dh), v_out[0].reshape(bp, sp, n_heads, dh),
            k_out[1].reshape(bs, ss, n_heads, dh), v_out[1].reshape(bs, ss, n_heads, dh),
            jnp.stack(chunk_v))
```

```python
import functools
import math

import jax
import jax.numpy as jnp
from jax import lax
from jax.experimental import pallas as pl
from jax.experimental.pallas import tpu as pltpu

F32 = jnp.float32
BF16 = jnp.bfloat16
LN_EPS = 1e-5
V7X_VMEM_LIMIT_BYTES = 56 * 1024 * 1024
LANES = 128
SUBLANES = 8
TOP_K = 2
ROUTE_WIDTH = LANES
EPILOGUE_ROWS = 128
DMA_UNROLL = 8
PAGES_PER_STEP = 4
SAMPLE_ROWS = 2 * SUBLANES


def _cparams(semantics):
    return pltpu.CompilerParams(dimension_semantics=semantics,
                                vmem_limit_bytes=V7X_VMEM_LIMIT_BYTES)


def _layer_norm(x, g, b):
    mu = jnp.mean(x, axis=-1, keepdims=True)
    xc = x - mu
    var = jnp.mean(xc * xc, axis=-1, keepdims=True)
    return xc * lax.rsqrt(var + LN_EPS) * g + b


def _gelu_tanh(x):
    c = math.sqrt(2.0 / math.pi)
    return x * (0.5 * (1.0 + jnp.tanh(c * (x + 0.044715 * (x * x * x)))))


def _log_sigmoid(z):
    return jnp.minimum(z, 0.0) - jnp.log(1.0 + jnp.exp(-jnp.abs(z)))


def _mod_spec(per_token, ts, tiles_per_seq, d):
    if per_token:
        return pl.BlockSpec((1, ts, d), lambda i, *_: (0, i, 0))
    return pl.BlockSpec((1, 1, d), lambda i, *_: (i // tiles_per_seq, 0, 0))


def _mod_kernel(c_ref, w_ref, b_ref, o_ref):
    c = c_ref[...].astype(BF16)
    w = w_ref[0].astype(BF16)
    o_ref[0] = jnp.dot(c, w, preferred_element_type=F32) + b_ref[0]


def _modulation(c_all, ada_w, ada_b):
    depth, d, d6 = ada_w.shape
    rows = c_all.shape[0]
    tn = 1024 if d6 % 1024 == 0 else d6
    return pl.pallas_call(
        _mod_kernel,
        grid=(depth, d6 // tn),
        in_specs=[pl.BlockSpec((rows, d), lambda l, n: (0, 0)),
                  pl.BlockSpec((1, d, tn), lambda l, n: (l, 0, n)),
                  pl.BlockSpec((1, 1, tn), lambda l, n: (l, 0, n))],
        out_specs=pl.BlockSpec((1, rows, tn), lambda l, n: (l, 0, n)),
        out_shape=jax.ShapeDtypeStruct((depth, rows, d6), F32),
        compiler_params=_cparams(("arbitrary", "arbitrary")),
        name="adaln_modulation",
    )(c_all, ada_w, ada_b.reshape(depth, 1, d6))


def _proj_kernel(*refs, n_out, modulate, has_bias, act, out_scale):
    refs = list(refs)
    x_ref = refs.pop(0)
    sc_ref = refs.pop(0) if modulate else None
    sh_ref = refs.pop(0) if modulate else None
    w_refs = [refs.pop(0) for _ in range(n_out)]
    b_refs = [refs.pop(0) for _ in range(n_out)] if has_bias else [None] * n_out
    o_refs = [refs.pop(0) for _ in range(n_out)]
    h_ref = refs.pop(0)

    @pl.when(pl.program_id(1) == 0)
    def _():
        x = x_ref[...]
        if modulate:
            x = x * (1.0 + sc_ref[0]) + sh_ref[0]
        h_ref[...] = x.astype(BF16)

    h = h_ref[...]
    for w_ref, b_ref, o_ref in zip(w_refs, b_refs, o_refs):
        z = jnp.dot(h, w_ref[...], preferred_element_type=F32)
        if has_bias:
            z = z + b_ref[...]
        if act:
            z = _gelu_tanh(z)
        if out_scale != 1.0:
            z = z * out_scale
        o_ref[...] = z.astype(o_ref.dtype)


def _projection(x, w, col_starts, width, out_dtypes, *, ts, tiles_per_seq, mod=None, bias=None,
                act=False, out_scale=1.0, name="projection"):
    n, d = x.shape
    tn = min(512, width)
    n_out = len(col_starts)
    modulate = mod is not None
    in_specs = [pl.BlockSpec((ts, d), lambda i, j: (i, 0))]
    args = [x]
    if modulate:
        sc, sh, per_token = mod
        in_specs += [_mod_spec(per_token, ts, tiles_per_seq, d)] * 2
        args += [sc, sh]
    for c0 in col_starts:
        in_specs.append(pl.BlockSpec((d, tn), functools.partial(lambda i, j, o: (0, j + o), o=c0 // tn)))
        args.append(w)
    if bias is not None:
        for c0 in col_starts:
            in_specs.append(pl.BlockSpec((1, tn), functools.partial(lambda i, j, o: (0, j + o), o=c0 // tn)))
            args.append(bias)
    return pl.pallas_call(
        functools.partial(_proj_kernel, n_out=n_out, modulate=modulate, has_bias=bias is not None, act=act,
                          out_scale=out_scale),
        grid=(n // ts, width // tn),
        in_specs=in_specs,
        out_specs=[pl.BlockSpec((ts, tn), lambda i, j: (i, j))] * n_out,
        out_shape=[jax.ShapeDtypeStruct((n, width), dt) for dt in out_dtypes],
        scratch_shapes=[pltpu.VMEM((ts, d), BF16)],
        compiler_params=_cparams(("parallel", "arbitrary")),
        name=name,
    )(*args)


def _gate_kernel(v_ref, u_ref, g_ref, b_ref, ws_ref, bs_ref, o_ref, *maybe_vn_ref, n_groups):
    v = v_ref[0]
    vn = _layer_norm(v, g_ref[...], b_ref[...])
    if maybe_vn_ref:
        maybe_vn_ref[0][0] = vn
    vb = vn.astype(BF16)
    rows, d_a = v.shape
    gd = d_a // n_groups
    causal = (lax.broadcasted_iota(jnp.int32, (rows, rows), 0)
              >= lax.broadcasted_iota(jnp.int32, (rows, rows), 1))
    for g in range(n_groups):
        w = jnp.where(causal, ws_ref[g], 0.0).astype(BF16)
        s = jnp.dot(w, vb[:, g * gd:(g + 1) * gd], preferred_element_type=F32) + bs_ref[:, g:g + 1]
        o_ref[0, :, g * gd:(g + 1) * gd] = (u_ref[0, :, g * gd:(g + 1) * gd].astype(F32) * s).astype(o_ref.dtype)


def _spatial_gate(v_raw, u, vn_g, vn_b, w_s, b_s_t, *, emit_vn):
    c, r, d_a = v_raw.shape
    n_groups = w_s.shape[0]
    blk = pl.BlockSpec((1, r, d_a), lambda i: (i, 0, 0))
    out_shape = [jax.ShapeDtypeStruct((c, r, d_a), BF16)]
    out_specs = [blk]
    if emit_vn:
        out_shape.append(jax.ShapeDtypeStruct((c, r, d_a), F32))
        out_specs.append(blk)
    return pl.pallas_call(
        functools.partial(_gate_kernel, n_groups=n_groups),
        grid=(c,),
        in_specs=[blk, blk,
                  pl.BlockSpec((1, d_a), lambda i: (0, 0)),
                  pl.BlockSpec((1, d_a), lambda i: (0, 0)),
                  pl.BlockSpec((n_groups, r, r), lambda i: (0, 0, 0)),
                  pl.BlockSpec((r, n_groups), lambda i: (0, 0))],
        out_specs=out_specs,
        out_shape=out_shape,
        compiler_params=_cparams(("parallel",)),
        name="spatial_gate",
    )(v_raw, u, vn_g, vn_b, w_s, b_s_t)


def _route(logits, n_groups, n_per):
    lane = lax.broadcasted_iota(jnp.int32, logits.shape, 1)
    lane_f = lane.astype(F32)
    neg = -jnp.inf
    big = float(2 * ROUTE_WIDTH)
    gl = jnp.where(lane < n_groups, logits, neg)
    gmax = jnp.max(gl, axis=-1, keepdims=True)
    g_sel = jnp.min(jnp.where(gl == gmax, lane_f, big), axis=-1, keepdims=True)
    g_weight = 1.0 / jnp.sum(jnp.exp(gl - gmax), axis=-1, keepdims=True)
    e_lane = lane_f - float(n_groups)
    in_group = (e_lane >= g_sel * n_per) & (e_lane < (g_sel + 1.0) * n_per)
    el = jnp.where(in_group, logits, neg)
    v1 = jnp.max(el, axis=-1, keepdims=True)
    i1 = jnp.min(jnp.where(el == v1, lane_f, big), axis=-1, keepdims=True)
    el2 = jnp.where(lane_f == i1, neg, el)
    v2 = jnp.max(el2, axis=-1, keepdims=True)
    i2 = jnp.min(jnp.where(el2 == v2, lane_f, big), axis=-1, keepdims=True)
    e2 = jnp.exp(v2 - v1)
    gate1 = g_weight * (1.0 / (1.0 + e2))
    gate2 = g_weight * (e2 / (1.0 + e2))
    id1 = i1 - float(n_groups)
    id2 = i2 - float(n_groups)
    return jnp.where(lane == 0, id1,
                     jnp.where(lane == 1, id2,
                               jnp.where(lane == 2, gate1,
                                         jnp.where(lane == 3, gate2, 0.0))))


def _outproj_kernel(a_ref, w_ref, x_ref, g1_ref, lng_ref, lnb_ref, sc2_ref, sh2_ref, wr_ref, br_ref,
                    x1_ref, h2_ref, rt_ref, acc_ref, *, alpha, n_groups, n_per):
    k = pl.program_id(1)
    nk = pl.num_programs(1)
    ts = acc_ref.shape[0]

    def epilogue(r):
        g1 = g1_ref[0] if g1_ref.shape[1] == 1 else g1_ref[0, r, :]
        sc2 = sc2_ref[0] if sc2_ref.shape[1] == 1 else sc2_ref[0, r, :]
        sh2 = sh2_ref[0] if sh2_ref.shape[1] == 1 else sh2_ref[0, r, :]
        t = alpha * x_ref[r, :] + (1.0 + g1) * acc_ref[r, :]
        x1 = _layer_norm(t, lng_ref[...], lnb_ref[...])
        x1_ref[r, :] = x1
        h2 = x1 * (1.0 + sc2) + sh2
        h2_ref[r, :] = h2
        logits = jnp.dot(h2.astype(BF16), wr_ref[...], preferred_element_type=F32) + br_ref[...]
        rt_ref[r, :] = _route(logits, n_groups, n_per)

    @pl.when(k == 0)
    def _():
        acc_ref[...] = jnp.zeros_like(acc_ref)

    acc_ref[...] += jnp.dot(a_ref[...], w_ref[...], preferred_element_type=F32)

    @pl.when(k == nk - 1)
    def _():
        rows = min(EPILOGUE_ROWS, ts)

        def slab(s, c):
            epilogue(pl.ds(pl.multiple_of(s * rows, rows), rows))
            return c

        lax.fori_loop(0, ts // rows, slab, 0)


def _out_projection(a, w, x, g1, ln_g, ln_b, sc2, sh2, w_route, b_route, *, per_token, ts, tiles_per_seq,
                    alpha, n_groups, n_per):
    n, kdim = a.shape
    d = w.shape[1]
    tk = min(512, kdim)
    mspec = _mod_spec(per_token, ts, tiles_per_seq, d)
    vec = pl.BlockSpec((1, d), lambda i, k: (0, 0))
    tile = pl.BlockSpec((ts, d), lambda i, k: (i, 0))
    return pl.pallas_call(
        functools.partial(_outproj_kernel, alpha=alpha, n_groups=n_groups, n_per=n_per),
        grid=(n // ts, kdim // tk),
        in_specs=[pl.BlockSpec((ts, tk), lambda i, k: (i, k)),
                  pl.BlockSpec((tk, d), lambda i, k: (k, 0)),
                  tile, mspec, vec, vec, mspec, mspec,
                  pl.BlockSpec((d, ROUTE_WIDTH), lambda i, k: (0, 0)),
                  pl.BlockSpec((1, ROUTE_WIDTH), lambda i, k: (0, 0))],
        out_specs=[tile, tile, pl.BlockSpec((ts, ROUTE_WIDTH), lambda i, k: (i, 0))],
        out_shape=[jax.ShapeDtypeStruct((n, d), F32), jax.ShapeDtypeStruct((n, d), F32),
                   jax.ShapeDtypeStruct((n, ROUTE_WIDTH), F32)],
        scratch_shapes=[pltpu.VMEM((ts, d), F32)],
        compiler_params=_cparams(("parallel", "arbitrary")),
        name="mixer_out_ln_route",
    )(a, w, x, g1, ln_g, ln_b, sc2, sh2, w_route, b_route)


def _moe_kernel(be_ref, nv_ref, na_ref, base_ref, ord_ref, ha_hbm, hb_hbm, wg_ref, wu_ref, wd_ref, out_hbm,
                xbuf, obuf, wgb, wub, wdb, gsem, ssem, *, tm, slot_stride, n_rows):
    i = pl.program_id(0)
    nb = pl.num_programs(0)
    slot = i % 2
    n_a = ha_hbm.shape[0]

    def gather_copy(blk, s, r, from_b):
        tok = ord_ref[base_ref[blk] + r] >> 1
        src = hb_hbm.at[pl.ds(tok - n_a, 1)] if from_b else ha_hbm.at[pl.ds(tok, 1)]
        return pltpu.make_async_copy(src, xbuf.at[s, pl.ds(r, 1)], gsem.at[s])

    def scatter_copy(blk, s, r):
        a = ord_ref[base_ref[blk] + r]
        dst = (a & 1) * slot_stride + (a >> 1)
        return pltpu.make_async_copy(obuf.at[s, pl.ds(r, 1)], out_hbm.at[pl.ds(dst, 1)], ssem.at[s])

    def for_rows(n, fn):
        n_groups = n // DMA_UNROLL

        def group(gidx, c):
            for u in range(DMA_UNROLL):
                fn(gidx * DMA_UNROLL + u)
            return c

        def tail(r, c):
            fn(r)
            return c

        lax.fori_loop(0, n_groups, group, 0)
        lax.fori_loop(n_groups * DMA_UNROLL, n, tail, 0)

    def start_gather(blk, s):
        n_from_a = na_ref[blk]
        for_rows(n_from_a, lambda r: gather_copy(blk, s, r, False).start())

        def from_b(r, c):
            gather_copy(blk, s, r, True).start()
            return c

        lax.fori_loop(n_from_a, nv_ref[blk], from_b, 0)

    def wait_rows(n, s, gather):
        def copy(rows):
            if gather:
                return pltpu.make_async_copy(ha_hbm.at[rows], xbuf.at[s, rows], gsem.at[s])
            return pltpu.make_async_copy(obuf.at[s, rows], out_hbm.at[rows], ssem.at[s])

        n_aligned = pl.multiple_of((n // SUBLANES) * SUBLANES, SUBLANES)

        @pl.when(n_aligned > 0)
        def _():
            copy(pl.ds(0, n_aligned)).wait()

        def tail(r, c):
            copy(pl.ds(0, 1)).wait()
            return c

        lax.fori_loop(n_aligned, n, tail, 0)

    prev = jnp.maximum(i - 1, 0)

    @pl.when(i == 0)
    def _():
        obuf[...] = jnp.zeros_like(obuf)
        for half in range(TOP_K):
            for start in range(n_rows, slot_stride, tm):
                size = min(tm, slot_stride - start)
                pltpu.sync_copy(obuf.at[0, pl.ds(0, size)], out_hbm.at[pl.ds(half * slot_stride + start, size)])
        xbuf[...] = jnp.zeros_like(xbuf)
        start_gather(0, 0)

    @pl.when(i + 1 < nb)
    def _():
        start_gather(jnp.minimum(i + 1, nb - 1), 1 - slot)

    @pl.when(i >= 2)
    def _():
        wait_rows(nv_ref[jnp.maximum(i - 2, 0)], slot, False)

    @pl.when(nv_ref[i] > 0)
    def _():
        wait_rows(nv_ref[i], slot, True)

        @pl.when((i == 0) | (be_ref[i] != be_ref[prev]))
        def _():
            wgb[...] = wg_ref[0, 0].astype(BF16)
            wub[...] = wu_ref[0, 0].astype(BF16)
            wdb[...] = wd_ref[0, 0].astype(BF16)

        x = xbuf[slot].astype(BF16)
        g = jnp.dot(x, wgb[...], preferred_element_type=F32)
        u = jnp.dot(x, wub[...], preferred_element_type=F32)
        mid = (g * jax.nn.sigmoid(g) * u).astype(BF16)
        o = jnp.dot(mid, wdb[...], preferred_element_type=F32)
        obuf[slot] = o
        for_rows(nv_ref[i], lambda r: scatter_copy(i, slot, r).start())

    @pl.when(i == nb - 1)
    def _():
        @pl.when(i >= 1)
        def _():
            wait_rows(nv_ref[prev], 1 - slot, False)
        wait_rows(nv_ref[i], slot, False)


def _moe_experts(h2_a, h2_b, blk_expert, blk_nvalid, blk_from_a, blk_base, order, w_gate, w_up, w_down, layer,
                 *, tm, slot_stride):
    d = h2_a.shape[1]
    n = h2_a.shape[0] + h2_b.shape[0]
    _, n_experts, _, de = w_gate.shape
    nb = blk_expert.shape[0]

    def expert_block(i, be, nv, na, base, order):
        return (layer, be[i], 0, 0)

    grid_spec = pltpu.PrefetchScalarGridSpec(
        num_scalar_prefetch=5,
        grid=(nb,),
        in_specs=[pl.BlockSpec(memory_space=pl.ANY), pl.BlockSpec(memory_space=pl.ANY),
                  pl.BlockSpec((1, 1, d, de), expert_block),
                  pl.BlockSpec((1, 1, d, de), expert_block),
                  pl.BlockSpec((1, 1, de, d), expert_block)],
        out_specs=pl.BlockSpec(memory_space=pl.ANY),
        scratch_shapes=[pltpu.VMEM((2, tm, d), F32), pltpu.VMEM((2, tm, d), F32),
                        pltpu.VMEM((d, de), BF16), pltpu.VMEM((d, de), BF16), pltpu.VMEM((de, d), BF16),
                        pltpu.SemaphoreType.DMA((2,)), pltpu.SemaphoreType.DMA((2,))],
    )
    return pl.pallas_call(
        functools.partial(_moe_kernel, tm=tm, slot_stride=slot_stride, n_rows=n),
        grid_spec=grid_spec,
        out_shape=jax.ShapeDtypeStruct((TOP_K * slot_stride, d), F32),
        compiler_params=_cparams(("arbitrary",)),
        name="moe_experts",
    )(blk_expert, blk_nvalid, blk_from_a, blk_base, order, h2_a, h2_b, w_gate, w_up, w_down)


def _dispatch(route, n_experts, tm, n_first):
    n = route.shape[0]
    eid = route[:, :TOP_K].astype(jnp.int32).reshape(-1)
    n_assign = n * TOP_K
    nb = -(-n_assign // tm) + n_experts
    _, order = lax.sort((eid, jnp.arange(n_assign, dtype=jnp.int32)), num_keys=1)
    experts = jnp.arange(n_experts, dtype=jnp.int32)
    starts = jnp.sum(eid[None, :] < experts[:, None], axis=1, dtype=jnp.int32)
    counts = jnp.sum(eid[None, :] == experts[:, None], axis=1, dtype=jnp.int32)
    counts_first = jnp.sum(eid[None, :n_first * TOP_K] == experts[:, None], axis=1, dtype=jnp.int32)
    nblk = (counts + tm - 1) // tm
    blk_end = jnp.cumsum(nblk)
    blk_start = blk_end - nblk
    blk = jnp.arange(nb, dtype=jnp.int32)
    n_used = blk_end[-1]
    be = jnp.minimum(jnp.sum(blk_end[None, :] <= blk[:, None], axis=1, dtype=jnp.int32), n_experts - 1)
    mine = be[:, None] == experts[None, :]

    def of_expert(per_expert):
        return jnp.sum(jnp.where(mine, per_expert[None, :], 0), axis=1, dtype=jnp.int32)

    used = blk < n_used
    first_row = (blk - of_expert(blk_start)) * tm
    nv = jnp.where(used, jnp.clip(of_expert(counts) - first_row, 0, tm), 0).astype(jnp.int32)
    n_from_first = jnp.clip(of_expert(counts_first) - first_row, 0, nv).astype(jnp.int32)
    base = jnp.where(used, of_expert(starts) + first_row, 0).astype(jnp.int32)
    last_e = jnp.max(jnp.where(used, be, 0))
    be = jnp.where(used, be, last_e)
    return be, nv, n_from_first, base, order


def _combine_kernel(x1_ref, oa_ref, ob_ref, rt_ref, g2_ref, lng_ref, lnb_ref, x2_ref, *, alpha):
    y = oa_ref[...] * rt_ref[:, TOP_K:TOP_K + 1] + ob_ref[...] * rt_ref[:, TOP_K + 1:TOP_K + 2]
    t = alpha * x1_ref[...] + (1.0 + g2_ref[0]) * y
    x2_ref[...] = _layer_norm(t, lng_ref[...], lnb_ref[...])


def _combine(x1, out2, route, first_row, slot_stride, g2, ln_g, ln_b, *, per_token, ts, tiles_per_seq, alpha):
    n, d = x1.shape
    vec = pl.BlockSpec((1, d), lambda i: (0, 0))
    tile = pl.BlockSpec((ts, d), lambda i: (i, 0))
    off_a = first_row // ts
    off_b = (first_row + slot_stride) // ts
    return pl.pallas_call(
        functools.partial(_combine_kernel, alpha=alpha),
        grid=(n // ts,),
        in_specs=[tile,
                  pl.BlockSpec((ts, d), lambda i: (i + off_a, 0)),
                  pl.BlockSpec((ts, d), lambda i: (i + off_b, 0)),
                  pl.BlockSpec((ts, ROUTE_WIDTH), lambda i: (i, 0)),
                  _mod_spec(per_token, ts, tiles_per_seq, d), vec, vec],
        out_specs=tile,
        out_shape=jax.ShapeDtypeStruct((n, d), F32),
        compiler_params=_cparams(("parallel",)),
        name="moe_combine_ln",
    )(x1, out2, out2, route, g2, ln_g, ln_b)


def _split2(x):
    hi = x.astype(BF16)
    return hi, (x - hi.astype(F32)).astype(BF16)


def _stick_terms(z, vis, excl_ones):
    lb = _log_sigmoid(z)
    ls = lb - z
    if vis is not None:
        ls = jnp.where(vis, ls, 0.0)
    hi, lo = _split2(ls)
    later = jnp.dot(hi, excl_ones, preferred_element_type=F32) + jnp.dot(lo, excl_ones, preferred_element_type=F32)
    return lb, later, jnp.sum(ls, axis=-1, keepdims=True)


def _stick_weights(lb, later, run, vis):
    a = jnp.exp(lb + (later + run))
    if vis is not None:
        a = jnp.where(vis, a, 0.0)
    return a


def _stick_block(z, vis, excl_ones, run):
    lb, later, total = _stick_terms(z, vis, excl_ones)
    return _stick_weights(lb, later, run, vis), run + total


def _excl_ones(tk):
    j = lax.broadcasted_iota(jnp.int32, (tk, tk), 0)
    s = lax.broadcasted_iota(jnp.int32, (tk, tk), 1)
    return jnp.where(j > s, 1.0, 0.0).astype(BF16)


def _attn_kernel(bias_ref, q_ref, k_ref, v_ref, o_ref, *, t, dh, heads):
    scale = dh ** -0.5
    hg = pl.program_id(1)
    qi = pl.program_id(2)
    ones = _excl_ones(t)
    below = (lax.broadcasted_iota(jnp.int32, (t, t), 1) < lax.broadcasted_iota(jnp.int32, (t, t), 0))

    def blocks(starts, carry, masks):
        out = []
        for hh in range(heads):
            acc, run = carry[hh]
            lanes = slice(hh * dh, (hh + 1) * dh)
            terms = []
            for start, vis in zip(starts, masks):
                k = k_ref[0, pl.ds(start, t), lanes].astype(BF16)
                z = lax.dot_general(q_ref[0, :, lanes], k, (((1,), (1,)), ((), ())),
                                    preferred_element_type=F32) * scale + bias_ref[hg * heads + hh]
                terms.append(_stick_terms(z, vis, ones))
            for start, vis, (lb, later, total) in zip(starts, masks, terms):
                a = _stick_weights(lb, later, run, vis)
                run = run + total
                v = v_ref[0, pl.ds(start, t), lanes].astype(BF16)
                acc = acc + jnp.dot(a.astype(BF16), v, preferred_element_type=F32)
            out.append((acc, run))
        return tuple(out)

    init = tuple((jnp.zeros((t, dh), F32), jnp.zeros((t, 1), F32)) for _ in range(heads))
    diag = pl.multiple_of(qi * t, t)
    carry = lax.cond(qi == 0,
                     lambda: blocks([diag], init, [below]),
                     lambda: blocks([diag, pl.multiple_of(diag - t, t)], init, [below, None]))
    rest = jnp.maximum(qi - 1, 0)

    def pair(j, c):
        newer = pl.multiple_of((rest - 1 - 2 * j) * t, t)
        return blocks([newer, pl.multiple_of(newer - t, t)], c, [None, None])

    carry = lax.fori_loop(0, rest // 2, pair, carry)
    carry = lax.fori_loop(0, rest % 2, lambda j, c: blocks([0], c, [None]), carry)
    for hh in range(heads):
        o_ref[0, :, hh * dh:(hh + 1) * dh] = carry[hh][0].astype(o_ref.dtype)


def _attention_prompt(q, k, v, logit_bias, *, n_heads):
    b, s, hd = q.shape
    dh = hd // n_heads
    t = min(256, s)
    heads = 2 if n_heads % 2 == 0 else 1
    w = heads * dh
    return pl.pallas_call(
        functools.partial(_attn_kernel, t=t, dh=dh, heads=heads),
        grid=(b, n_heads // heads, s // t),
        in_specs=[pl.BlockSpec(memory_space=pltpu.SMEM),
                  pl.BlockSpec((1, t, w), lambda bi, h, qi: (bi, qi, h)),
                  pl.BlockSpec((1, s, w), lambda bi, h, qi: (bi, 0, h)),
                  pl.BlockSpec((1, s, w), lambda bi, h, qi: (bi, 0, h))],
        out_specs=pl.BlockSpec((1, t, w), lambda bi, h, qi: (bi, qi, h)),
        out_shape=jax.ShapeDtypeStruct((b, s, hd), BF16),
        compiler_params=_cparams(("parallel", "parallel", "arbitrary")),
        name="stick_attention_prompt",
    )(logit_bias, q, k, v)


def _attn_paged_kernel(pt_ref, q_ref, bias_ref, own_ref, pick_ref, spread_ref, *rest, page, nq, pps):
    k_refs, v_refs = rest[:pps], rest[pps:2 * pps]
    kn_ref, vn_ref, o_ref, acc_ref, run_ref = rest[2 * pps:]
    step = pl.program_id(1)
    scale = q_ref.shape[-1] ** -0.5

    @pl.when(step == 0)
    def _():
        acc_ref[...] = jnp.zeros_like(acc_ref)
        run_ref[...] = jnp.zeros_like(run_ref)

    def process(pages, new_keys):
        own = own_ref[...]
        ones = _excl_ones(page)
        terms = []
        for k_ref, _ in pages:
            s_all = lax.dot_general(q_ref[0], k_ref[0].astype(BF16), (((1,), (1,)), ((), ())),
                                    preferred_element_type=F32)
            hi, lo = _split2(s_all * own)
            z = (jnp.dot(hi, pick_ref[...], preferred_element_type=F32)
                 + jnp.dot(lo, pick_ref[...], preferred_element_type=F32)) * scale + bias_ref[...]
            vis = None
            if new_keys:
                t = lax.broadcasted_iota(jnp.int32, z.shape, 0) % nq
                vis = lax.broadcasted_iota(jnp.int32, z.shape, 1) < t
            terms.append(_stick_terms(z, vis, ones) + (vis,))
        run = run_ref[...]
        acc = acc_ref[...]
        for (lb, later, total, vis), (_, v_ref) in zip(terms, pages):
            a = _stick_weights(lb, later, run, vis)
            run = run + total
            a_all = (jnp.dot(a.astype(BF16), spread_ref[...], preferred_element_type=F32) * own).astype(BF16)
            acc = acc + jnp.dot(a_all, v_ref[0].astype(BF16), preferred_element_type=F32)
        run_ref[...] = run
        acc_ref[...] = acc

    @pl.when(step == 0)
    def _():
        process([(kn_ref, vn_ref)], True)

    @pl.when(step > 0)
    def _():
        process(list(zip(k_refs, v_refs)), False)

    @pl.when(step == pl.num_programs(1) - 1)
    def _():
        o_ref[0] = acc_ref[...]


def _attention_paged(q_rows, bias_col, cache_k, cache_v, k_new, v_new, page_table, *, n_heads, nq):
    b, rows, dh = q_rows.shape
    n_pages = page_table.shape[1]
    cols = cache_k.shape[1]
    page = cols // n_heads
    col = jnp.arange(cols, dtype=jnp.int32)
    own = (col[None, :] % n_heads == jnp.arange(rows, dtype=jnp.int32)[:, None] // nq).astype(F32)
    pick = (col[:, None] // n_heads == jnp.arange(page, dtype=jnp.int32)[None, :]).astype(BF16)

    pps = next(p for p in (PAGES_PER_STEP, 2, 1) if n_pages % p == 0)

    def page_spec(j):
        return pl.BlockSpec(
            (1, cols, dh), lambda bi, s, pt: (pt[bi, n_pages - 1 - (jnp.maximum(s, 1) - 1) * pps - j], 0, 0))

    def const(shape):
        return pl.BlockSpec(shape, lambda bi, s, pt: (0,) * len(shape))

    grid_spec = pltpu.PrefetchScalarGridSpec(
        num_scalar_prefetch=1,
        grid=(b, n_pages // pps + 1),
        in_specs=[pl.BlockSpec((1, rows, dh), lambda bi, s, pt: (bi, 0, 0)),
                  const((rows, 1)), const((rows, cols)), const((cols, page)), const((page, cols))]
                 + [page_spec(j) for j in range(pps)] * 2
                 + [pl.BlockSpec((1, cols, dh), lambda bi, s, pt: (bi, 0, 0))] * 2,
        out_specs=pl.BlockSpec((1, rows, dh), lambda bi, s, pt: (bi, 0, 0)),
        scratch_shapes=[pltpu.VMEM((rows, dh), F32), pltpu.VMEM((rows, 1), F32)],
    )
    return pl.pallas_call(
        functools.partial(_attn_paged_kernel, page=page, nq=nq, pps=pps),
        grid_spec=grid_spec,
        out_shape=jax.ShapeDtypeStruct((b, rows, dh), F32),
        compiler_params=_cparams(("parallel", "arbitrary")),
        name="stick_attention_paged",
    )(page_table, q_rows, bias_col, own, pick, jnp.transpose(pick), *([cache_k] * pps), *([cache_v] * pps),
      k_new, v_new)


def kernel(x_prompt, x_sample, c_prompt, c_sample, cache_k, cache_v, page_table, ada_w, ada_b, ln1_g, ln1_b, ln2_g, ln2_b, a_w_in, a_b_in, a_vnorm_g, a_vnorm_b, a_w_s, a_b_s, a_w_out, kv_w, b_w_q, b_w_o, b_logit_bias, moe_router_group_w, moe_router_group_b, moe_router_expert_w, moe_router_expert_b, moe_w_gate, moe_w_up, moe_w_down):
    bp, sp, d = x_prompt.shape
    bs, ss, _ = x_sample.shape
    depth = ada_w.shape[0]
    n_a = a_w_in.shape[0]
    d_a = a_w_out.shape[1]
    chunk = a_w_s.shape[-1]
    n_pool, page, n_heads, dh = cache_k.shape
    hd = n_heads * dh
    n_groups, n_per = moe_router_expert_b.shape[1:]
    n_experts = n_groups * n_per
    alpha = (2 * depth) ** 0.25
    assert n_groups + n_experts <= ROUTE_WIDTH and ss <= SUBLANES and sp % chunk == 0
    np_tok, ns_tok = bp * sp, bs * ss
    ts_p = min(512, sp)
    tiles_p = sp // ts_p
    assert np_tok % ns_tok == 0

    c_rows = -(-(bp + bs) // SUBLANES) * SUBLANES
    c_all = jnp.concatenate([c_prompt, c_sample, jnp.zeros((c_rows - bp - bs, d), F32)], axis=0)
    mod = _modulation(c_all, ada_w, ada_b)

    def mods(l, group):
        m = mod[l].reshape(c_rows, 6, d)
        if group == 0:
            return [m[:bp, j].reshape(bp, 1, d) for j in range(6)]
        return [jnp.repeat(m[bp:bp + bs, j], ss, axis=0).reshape(1, ns_tok, d) for j in range(6)]

    groups = [dict(x=x_prompt.reshape(np_tok, d), ts=ts_p, tiles=tiles_p, per_token=False, n=np_tok),
              dict(x=x_sample.reshape(ns_tok, d), ts=ns_tok, tiles=1, per_token=True, n=ns_tok)]
    tm = 128
    slot_stride = -(-(np_tok + ns_tok) // ts_p) * ts_p
    assert slot_stride % ns_tok == 0 and (n_heads * ss) % SUBLANES == 0
    chunk_v = []
    k_out, v_out = [None, None], [None, None]

    for l in range(depth):
        w_route = jnp.concatenate(
            [moe_router_group_w[l],
             jnp.transpose(moe_router_expert_w[l], (1, 0, 2)).reshape(d, n_experts),
             jnp.zeros((d, ROUTE_WIDTH - n_groups - n_experts), F32)], axis=1).astype(BF16)
        b_route = jnp.concatenate(
            [moe_router_group_b[l], moe_router_expert_b[l].reshape(-1),
             jnp.zeros((ROUTE_WIDTH - n_groups - n_experts,), F32)]).reshape(1, ROUTE_WIDTH)
        x1s, h2s, routes, g2s = [], [], [], []
        for gi, grp in enumerate(groups):
            sh1, sc1, g1, sh2, sc2, g2 = mods(l, gi)
            x, ts, tiles, per_token = grp["x"], grp["ts"], grp["tiles"], grp["per_token"]
            if l < n_a:
                u, v_raw = _projection(x, a_w_in[l].astype(BF16), (0, d_a), d_a, (F32, F32), ts=ts,
                                       tiles_per_seq=tiles, mod=(sc1, sh1, per_token),
                                       bias=a_b_in[l].reshape(1, 2 * d_a), act=True, name="gmlp_in")
                bs_t = jnp.transpose(a_b_s[l])
                if gi == 0:
                    n_chunks = np_tok // chunk
                    gated, = _spatial_gate(v_raw.reshape(n_chunks, chunk, d_a), u.reshape(n_chunks, chunk, d_a),
                                           a_vnorm_g[l].reshape(1, d_a), a_vnorm_b[l].reshape(1, d_a),
                                           a_w_s[l], bs_t, emit_vn=False)
                    a_mix = gated.reshape(np_tok, d_a)
                else:
                    pad = ((0, 0), (0, SAMPLE_ROWS - ss), (0, 0))
                    gated, vn = _spatial_gate(jnp.pad(v_raw.reshape(bs, ss, d_a), pad),
                                              jnp.pad(u.reshape(bs, ss, d_a), pad),
                                              a_vnorm_g[l].reshape(1, d_a), a_vnorm_b[l].reshape(1, d_a),
                                              a_w_s[l][:, :SAMPLE_ROWS, :SAMPLE_ROWS], bs_t[:SAMPLE_ROWS], emit_vn=True)
                    a_mix = gated[:, :ss].reshape(ns_tok, d_a)
                    chunk_v.append(vn[:, :ss])
                w_mix = a_w_out[l].astype(BF16)
            else:
                j = l - n_a
                if k_out[gi] is None:
                    k_out[gi], v_out[gi] = _projection(x, kv_w.astype(BF16), (0, hd), hd, (F32, F32), ts=ts,
                                                       tiles_per_seq=tiles, name="kv_proj")
                q, = _projection(x, b_w_q[j].astype(BF16), (0,), hd, (BF16,), ts=ts, tiles_per_seq=tiles,
                                 mod=(sc1, sh1, per_token), name="q_proj")
                if gi == 0:
                    o = _attention_prompt(q.reshape(bp, sp, hd), k_out[gi].reshape(bp, sp, hd),
                                          v_out[gi].reshape(bp, sp, hd), b_logit_bias[j], n_heads=n_heads)
                    a_mix = o.reshape(np_tok, hd)
                else:
                    q_rows = jnp.transpose(q.reshape(bs, ss, n_heads, dh), (0, 2, 1, 3)).reshape(bs, n_heads * ss, dh)

                    def new_page(t):
                        t = jnp.pad(t.reshape(bs, ss, n_heads, dh), ((0, 0), (0, page - ss), (0, 0), (0, 0)))
                        return t.reshape(bs, page * n_heads, dh)

                    o = _attention_paged(q_rows, jnp.repeat(b_logit_bias[j], ss).reshape(n_heads * ss, 1),
                                         cache_k.reshape(n_pool, page * n_heads, dh),
                                         cache_v.reshape(n_pool, page * n_heads, dh),
                                         new_page(k_out[gi]), new_page(v_out[gi]), page_table,
                                         n_heads=n_heads, nq=ss)
                    o = jnp.transpose(o.reshape(bs, n_heads, ss, dh), (0, 2, 1, 3))
                    a_mix = o.reshape(ns_tok, hd).astype(BF16)
                w_mix = b_w_o[j].astype(BF16)
            x1, h2, route = _out_projection(a_mix, w_mix, x, g1, ln1_g[l].reshape(1, d), ln1_b[l].reshape(1, d),
                                            sc2, sh2, w_route, b_route, per_token=per_token, ts=ts,
                                            tiles_per_seq=tiles, alpha=alpha, n_groups=n_groups, n_per=n_per)
            x1s.append(x1)
            h2s.append(h2)
            routes.append(route)
            g2s.append(g2)

        be, nv, n_prompt_rows, base, order = _dispatch(jnp.concatenate(routes, axis=0), n_experts, tm, np_tok)
        out2 = _moe_experts(h2s[0], h2s[1], be, nv, n_prompt_rows, base, order, moe_w_gate, moe_w_up,
                            moe_w_down, l, tm=tm, slot_stride=slot_stride)
        first_rows = [0, np_tok]
        for gi, grp in enumerate(groups):
            grp["x"] = _combine(x1s[gi], out2, routes[gi], first_rows[gi], slot_stride, g2s[gi],
                                ln2_g[l].reshape(1, d), ln2_b[l].reshape(1, d), per_token=grp["per_token"],
                                ts=grp["ts"], tiles_per_seq=grp["tiles"], alpha=alpha)

    y_prompt = groups[0]["x"].reshape(bp, sp, d)
    y_sample = groups[1]["x"].reshape(bs, ss, d)
    return (y_prompt, y_sample,
            k_out[0].reshape(bp, sp, n_heads, dh), v_out[0].reshape(bp, sp, n_heads, dh),
            k_out[1].reshape(bs, ss, n_heads, dh), v_out[1].reshape(bs, ss, n_heads, dh),
            jnp.stack(chunk_v))
```

```python
import functools
import math

import jax
import jax.numpy as jnp
from jax import lax
from jax.experimental import pallas as pl
from jax.experimental.pallas import tpu as pltpu

F32 = jnp.float32
BF16 = jnp.bfloat16
LN_EPS = 1e-5
V7X_VMEM_LIMIT_BYTES = 56 * 1024 * 1024
LANES = 128
SUBLANES = 8
TOP_K = 2
ROUTE_WIDTH = LANES
EPILOGUE_ROWS = 128
DMA_UNROLL = 8
PAGES_PER_STEP = 4
SAMPLE_ROWS = 2 * SUBLANES


def _cparams(semantics):
    return pltpu.CompilerParams(dimension_semantics=semantics,
                                vmem_limit_bytes=V7X_VMEM_LIMIT_BYTES)


def _layer_norm(x, g, b):
    mu = jnp.mean(x, axis=-1, keepdims=True)
    xc = x - mu
    var = jnp.mean(xc * xc, axis=-1, keepdims=True)
    return xc * lax.rsqrt(var + LN_EPS) * g + b


def _gelu_tanh(x):
    c = math.sqrt(2.0 / math.pi)
    return x * (0.5 * (1.0 + jnp.tanh(c * (x + 0.044715 * (x * x * x)))))


def _log_sigmoid(z):
    return jnp.minimum(z, 0.0) - jnp.log(1.0 + jnp.exp(-jnp.abs(z)))


def _mod_spec(per_token, ts, tiles_per_seq, d):
    if per_token:
        return pl.BlockSpec((1, ts, d), lambda i, *_: (0, i, 0))
    return pl.BlockSpec((1, 1, d), lambda i, *_: (i // tiles_per_seq, 0, 0))


def _mod_kernel(c_ref, w_ref, b_ref, o_ref):
    c = c_ref[...].astype(BF16)
    w = w_ref[0].astype(BF16)
    o_ref[0] = jnp.dot(c, w, preferred_element_type=F32) + b_ref[0]


def _modulation(c_all, ada_w, ada_b):
    depth, d, d6 = ada_w.shape
    rows = c_all.shape[0]
    tn = 1024 if d6 % 1024 == 0 else d6
    return pl.pallas_call(
        _mod_kernel,
        grid=(depth, d6 // tn),
        in_specs=[pl.BlockSpec((rows, d), lambda l, n: (0, 0)),
                  pl.BlockSpec((1, d, tn), lambda l, n: (l, 0, n)),
                  pl.BlockSpec((1, 1, tn), lambda l, n: (l, 0, n))],
        out_specs=pl.BlockSpec((1, rows, tn), lambda l, n: (l, 0, n)),
        out_shape=jax.ShapeDtypeStruct((depth, rows, d6), F32),
        compiler_params=_cparams(("arbitrary", "arbitrary")),
        name="adaln_modulation",
    )(c_all, ada_w, ada_b.reshape(depth, 1, d6))


def _proj_kernel(*refs, n_out, modulate, has_bias, act, out_scale):
    refs = list(refs)
    x_ref = refs.pop(0)
    sc_ref = refs.pop(0) if modulate else None
    sh_ref = refs.pop(0) if modulate else None
    w_refs = [refs.pop(0) for _ in range(n_out)]
    b_refs = [refs.pop(0) for _ in range(n_out)] if has_bias else [None] * n_out
    o_refs = [refs.pop(0) for _ in range(n_out)]
    h_ref = refs.pop(0)

    @pl.when(pl.program_id(1) == 0)
    def _():
        x = x_ref[...]
        if modulate:
            x = x * (1.0 + sc_ref[0]) + sh_ref[0]
        h_ref[...] = x.astype(BF16)

    h = h_ref[...]
    for w_ref, b_ref, o_ref in zip(w_refs, b_refs, o_refs):
        z = jnp.dot(h, w_ref[...], preferred_element_type=F32)
        if has_bias:
            z = z + b_ref[...]
        if act:
            z = _gelu_tanh(z)
        if out_scale != 1.0:
            z = z * out_scale
        o_ref[...] = z.astype(o_ref.dtype)


def _projection(x, w, col_starts, width, out_dtypes, *, ts, tiles_per_seq, mod=None, bias=None,
                act=False, out_scale=1.0, name="projection"):
    n, d = x.shape
    tn = next(t for t in (1024, 512, width) if width % t == 0)
    n_out = len(col_starts)
    modulate = mod is not None
    in_specs = [pl.BlockSpec((ts, d), lambda i, j: (i, 0))]
    args = [x]
    if modulate:
        sc, sh, per_token = mod
        in_specs += [_mod_spec(per_token, ts, tiles_per_seq, d)] * 2
        args += [sc, sh]
    for c0 in col_starts:
        in_specs.append(pl.BlockSpec((d, tn), functools.partial(lambda i, j, o: (0, j + o), o=c0 // tn)))
        args.append(w)
    if bias is not None:
        for c0 in col_starts:
            in_specs.append(pl.BlockSpec((1, tn), functools.partial(lambda i, j, o: (0, j + o), o=c0 // tn)))
            args.append(bias)
    return pl.pallas_call(
        functools.partial(_proj_kernel, n_out=n_out, modulate=modulate, has_bias=bias is not None, act=act,
                          out_scale=out_scale),
        grid=(n // ts, width // tn),
        in_specs=in_specs,
        out_specs=[pl.BlockSpec((ts, tn), lambda i, j: (i, j))] * n_out,
        out_shape=[jax.ShapeDtypeStruct((n, width), dt) for dt in out_dtypes],
        scratch_shapes=[pltpu.VMEM((ts, d), BF16)],
        compiler_params=_cparams(("parallel", "arbitrary")),
        name=name,
    )(*args)


def _gate_kernel(v_ref, u_ref, g_ref, b_ref, ws_ref, bs_ref, o_ref, *maybe_vn_ref, n_groups):
    v = v_ref[0]
    vn = _layer_norm(v, g_ref[...], b_ref[...])
    if maybe_vn_ref:
        maybe_vn_ref[0][0] = vn
    vb = vn.astype(BF16)
    rows, d_a = v.shape
    gd = d_a // n_groups
    causal = (lax.broadcasted_iota(jnp.int32, (rows, rows), 0)
              >= lax.broadcasted_iota(jnp.int32, (rows, rows), 1))
    for g in range(n_groups):
        w = jnp.where(causal, ws_ref[g], 0.0).astype(BF16)
        s = jnp.dot(w, vb[:, g * gd:(g + 1) * gd], preferred_element_type=F32) + bs_ref[:, g:g + 1]
        o_ref[0, :, g * gd:(g + 1) * gd] = (u_ref[0, :, g * gd:(g + 1) * gd].astype(F32) * s).astype(o_ref.dtype)


def _spatial_gate(v_raw, u, vn_g, vn_b, w_s, b_s_t, *, emit_vn):
    c, r, d_a = v_raw.shape
    n_groups = w_s.shape[0]
    blk = pl.BlockSpec((1, r, d_a), lambda i: (i, 0, 0))
    out_shape = [jax.ShapeDtypeStruct((c, r, d_a), BF16)]
    out_specs = [blk]
    if emit_vn:
        out_shape.append(jax.ShapeDtypeStruct((c, r, d_a), F32))
        out_specs.append(blk)
    return pl.pallas_call(
        functools.partial(_gate_kernel, n_groups=n_groups),
        grid=(c,),
        in_specs=[blk, blk,
                  pl.BlockSpec((1, d_a), lambda i: (0, 0)),
                  pl.BlockSpec((1, d_a), lambda i: (0, 0)),
                  pl.BlockSpec((n_groups, r, r), lambda i: (0, 0, 0)),
                  pl.BlockSpec((r, n_groups), lambda i: (0, 0))],
        out_specs=out_specs,
        out_shape=out_shape,
        compiler_params=_cparams(("parallel",)),
        name="spatial_gate",
    )(v_raw, u, vn_g, vn_b, w_s, b_s_t)


def _route(logits, n_groups, n_per):
    lane = lax.broadcasted_iota(jnp.int32, logits.shape, 1)
    lane_f = lane.astype(F32)
    neg = -jnp.inf
    big = float(2 * ROUTE_WIDTH)
    gl = jnp.where(lane < n_groups, logits, neg)
    gmax = jnp.max(gl, axis=-1, keepdims=True)
    g_sel = jnp.min(jnp.where(gl == gmax, lane_f, big), axis=-1, keepdims=True)
    g_weight = 1.0 / jnp.sum(jnp.exp(gl - gmax), axis=-1, keepdims=True)
    e_lane = lane_f - float(n_groups)
    in_group = (e_lane >= g_sel * n_per) & (e_lane < (g_sel + 1.0) * n_per)
    el = jnp.where(in_group, logits, neg)
    v1 = jnp.max(el, axis=-1, keepdims=True)
    i1 = jnp.min(jnp.where(el == v1, lane_f, big), axis=-1, keepdims=True)
    el2 = jnp.where(lane_f == i1, neg, el)
    v2 = jnp.max(el2, axis=-1, keepdims=True)
    i2 = jnp.min(jnp.where(el2 == v2, lane_f, big), axis=-1, keepdims=True)
    e2 = jnp.exp(v2 - v1)
    gate1 = g_weight * (1.0 / (1.0 + e2))
    gate2 = g_weight * (e2 / (1.0 + e2))
    id1 = i1 - float(n_groups)
    id2 = i2 - float(n_groups)
    return jnp.where(lane == 0, id1,
                     jnp.where(lane == 1, id2,
                               jnp.where(lane == 2, gate1,
                                         jnp.where(lane == 3, gate2, 0.0))))


def _outproj_kernel(a_ref, w_ref, x_ref, g1_ref, lng_ref, lnb_ref, sc2_ref, sh2_ref, wr_ref, br_ref,
                    x1_ref, h2_ref, rt_ref, acc_ref, *, alpha, n_groups, n_per):
    k = pl.program_id(1)
    nk = pl.num_programs(1)
    ts = acc_ref.shape[0]

    def epilogue(r):
        g1 = g1_ref[0] if g1_ref.shape[1] == 1 else g1_ref[0, r, :]
        sc2 = sc2_ref[0] if sc2_ref.shape[1] == 1 else sc2_ref[0, r, :]
        sh2 = sh2_ref[0] if sh2_ref.shape[1] == 1 else sh2_ref[0, r, :]
        t = alpha * x_ref[r, :] + (1.0 + g1) * acc_ref[r, :]
        x1 = _layer_norm(t, lng_ref[...], lnb_ref[...])
        x1_ref[r, :] = x1
        h2 = x1 * (1.0 + sc2) + sh2
        h2_ref[r, :] = h2
        logits = jnp.dot(h2.astype(BF16), wr_ref[...], preferred_element_type=F32) + br_ref[...]
        rt_ref[r, :] = _route(logits, n_groups, n_per)

    @pl.when(k == 0)
    def _():
        acc_ref[...] = jnp.zeros_like(acc_ref)

    acc_ref[...] += jnp.dot(a_ref[...], w_ref[...], preferred_element_type=F32)

    @pl.when(k == nk - 1)
    def _():
        rows = min(EPILOGUE_ROWS, ts)

        def slab(s, c):
            epilogue(pl.ds(pl.multiple_of(s * rows, rows), rows))
            return c

        lax.fori_loop(0, ts // rows, slab, 0)


def _out_projection(a, w, x, g1, ln_g, ln_b, sc2, sh2, w_route, b_route, *, per_token, ts, tiles_per_seq,
                    alpha, n_groups, n_per):
    n, kdim = a.shape
    d = w.shape[1]
    tk = min(512, kdim)
    mspec = _mod_spec(per_token, ts, tiles_per_seq, d)
    vec = pl.BlockSpec((1, d), lambda i, k: (0, 0))
    tile = pl.BlockSpec((ts, d), lambda i, k: (i, 0))
    return pl.pallas_call(
        functools.partial(_outproj_kernel, alpha=alpha, n_groups=n_groups, n_per=n_per),
        grid=(n // ts, kdim // tk),
        in_specs=[pl.BlockSpec((ts, tk), lambda i, k: (i, k)),
                  pl.BlockSpec((tk, d), lambda i, k: (k, 0)),
                  tile, mspec, vec, vec, mspec, mspec,
                  pl.BlockSpec((d, ROUTE_WIDTH), lambda i, k: (0, 0)),
                  pl.BlockSpec((1, ROUTE_WIDTH), lambda i, k: (0, 0))],
        out_specs=[tile, tile, pl.BlockSpec((ts, ROUTE_WIDTH), lambda i, k: (i, 0))],
        out_shape=[jax.ShapeDtypeStruct((n, d), F32), jax.ShapeDtypeStruct((n, d), F32),
                   jax.ShapeDtypeStruct((n, ROUTE_WIDTH), F32)],
        scratch_shapes=[pltpu.VMEM((ts, d), F32)],
        compiler_params=_cparams(("parallel", "arbitrary")),
        name="mixer_out_ln_route",
    )(a, w, x, g1, ln_g, ln_b, sc2, sh2, w_route, b_route)


def _moe_kernel(be_ref, nv_ref, na_ref, base_ref, ord_ref, ha_hbm, hb_hbm, wg_ref, wu_ref, wd_ref, out_hbm,
                xbuf, obuf, wgb, wub, wdb, gsem, ssem, *, tm, slot_stride, n_rows):
    i = pl.program_id(0)
    nb = pl.num_programs(0)
    slot = i % 2
    n_a = ha_hbm.shape[0]

    def gather_copy(blk, s, r, from_b):
        tok = ord_ref[base_ref[blk] + r] >> 1
        src = hb_hbm.at[pl.ds(tok - n_a, 1)] if from_b else ha_hbm.at[pl.ds(tok, 1)]
        return pltpu.make_async_copy(src, xbuf.at[s, pl.ds(r, 1)], gsem.at[s])

    def scatter_copy(blk, s, r):
        a = ord_ref[base_ref[blk] + r]
        dst = (a & 1) * slot_stride + (a >> 1)
        return pltpu.make_async_copy(obuf.at[s, pl.ds(r, 1)], out_hbm.at[pl.ds(dst, 1)], ssem.at[s])

    def for_rows(n, fn):
        n_groups = n // DMA_UNROLL

        def group(gidx, c):
            for u in range(DMA_UNROLL):
                fn(gidx * DMA_UNROLL + u)
            return c

        def tail(r, c):
            fn(r)
            return c

        lax.fori_loop(0, n_groups, group, 0)
        lax.fori_loop(n_groups * DMA_UNROLL, n, tail, 0)

    def start_gather(blk, s):
        n_from_a = na_ref[blk]
        for_rows(n_from_a, lambda r: gather_copy(blk, s, r, False).start())

        def from_b(r, c):
            gather_copy(blk, s, r, True).start()
            return c

        lax.fori_loop(n_from_a, nv_ref[blk], from_b, 0)

    def wait_rows(n, s, gather):
        def copy(rows):
            if gather:
                return pltpu.make_async_copy(ha_hbm.at[rows], xbuf.at[s, rows], gsem.at[s])
            return pltpu.make_async_copy(obuf.at[s, rows], out_hbm.at[rows], ssem.at[s])

        n_aligned = pl.multiple_of((n // SUBLANES) * SUBLANES, SUBLANES)

        @pl.when(n_aligned > 0)
        def _():
            copy(pl.ds(0, n_aligned)).wait()

        def tail(r, c):
            copy(pl.ds(0, 1)).wait()
            return c

        lax.fori_loop(n_aligned, n, tail, 0)

    prev = jnp.maximum(i - 1, 0)

    @pl.when(i == 0)
    def _():
        obuf[...] = jnp.zeros_like(obuf)
        for half in range(TOP_K):
            for start in range(n_rows, slot_stride, tm):
                size = min(tm, slot_stride - start)
                pltpu.sync_copy(obuf.at[0, pl.ds(0, size)], out_hbm.at[pl.ds(half * slot_stride + start, size)])
        xbuf[...] = jnp.zeros_like(xbuf)
        start_gather(0, 0)

    @pl.when(i + 1 < nb)
    def _():
        start_gather(jnp.minimum(i + 1, nb - 1), 1 - slot)

    @pl.when(i >= 2)
    def _():
        wait_rows(nv_ref[jnp.maximum(i - 2, 0)], slot, False)

    @pl.when(nv_ref[i] > 0)
    def _():
        wait_rows(nv_ref[i], slot, True)

        @pl.when((i == 0) | (be_ref[i] != be_ref[prev]))
        def _():
            wgb[...] = wg_ref[0, 0].astype(BF16)
            wub[...] = wu_ref[0, 0].astype(BF16)
            wdb[...] = wd_ref[0, 0].astype(BF16)

        x = xbuf[slot].astype(BF16)
        g = jnp.dot(x, wgb[...], preferred_element_type=F32)
        u = jnp.dot(x, wub[...], preferred_element_type=F32)
        mid = (g * jax.nn.sigmoid(g) * u).astype(BF16)
        o = jnp.dot(mid, wdb[...], preferred_element_type=F32)
        obuf[slot] = o
        for_rows(nv_ref[i], lambda r: scatter_copy(i, slot, r).start())

    @pl.when(i == nb - 1)
    def _():
        @pl.when(i >= 1)
        def _():
            wait_rows(nv_ref[prev], 1 - slot, False)
        wait_rows(nv_ref[i], slot, False)


def _moe_experts(h2_a, h2_b, blk_expert, blk_nvalid, blk_from_a, blk_base, order, w_gate, w_up, w_down, layer,
                 *, tm, slot_stride):
    d = h2_a.shape[1]
    n = h2_a.shape[0] + h2_b.shape[0]
    _, n_experts, _, de = w_gate.shape
    nb = blk_expert.shape[0]

    def expert_block(i, be, nv, na, base, order):
        return (layer, be[i], 0, 0)

    grid_spec = pltpu.PrefetchScalarGridSpec(
        num_scalar_prefetch=5,
        grid=(nb,),
        in_specs=[pl.BlockSpec(memory_space=pl.ANY), pl.BlockSpec(memory_space=pl.ANY),
                  pl.BlockSpec((1, 1, d, de), expert_block),
                  pl.BlockSpec((1, 1, d, de), expert_block),
                  pl.BlockSpec((1, 1, de, d), expert_block)],
        out_specs=pl.BlockSpec(memory_space=pl.ANY),
        scratch_shapes=[pltpu.VMEM((2, tm, d), F32), pltpu.VMEM((2, tm, d), F32),
                        pltpu.VMEM((d, de), BF16), pltpu.VMEM((d, de), BF16), pltpu.VMEM((de, d), BF16),
                        pltpu.SemaphoreType.DMA((2,)), pltpu.SemaphoreType.DMA((2,))],
    )
    return pl.pallas_call(
        functools.partial(_moe_kernel, tm=tm, slot_stride=slot_stride, n_rows=n),
        grid_spec=grid_spec,
        out_shape=jax.ShapeDtypeStruct((TOP_K * slot_stride, d), F32),
        compiler_params=_cparams(("arbitrary",)),
        name="moe_experts",
    )(blk_expert, blk_nvalid, blk_from_a, blk_base, order, h2_a, h2_b, w_gate, w_up, w_down)


def _dispatch(route, n_experts, tm, n_first):
    n = route.shape[0]
    eid = route[:, :TOP_K].astype(jnp.int32).reshape(-1)
    n_assign = n * TOP_K
    nb = -(-n_assign // tm) + n_experts
    _, order = lax.sort((eid, jnp.arange(n_assign, dtype=jnp.int32)), num_keys=1)
    experts = jnp.arange(n_experts, dtype=jnp.int32)
    starts = jnp.sum(eid[None, :] < experts[:, None], axis=1, dtype=jnp.int32)
    counts = jnp.sum(eid[None, :] == experts[:, None], axis=1, dtype=jnp.int32)
    counts_first = jnp.sum(eid[None, :n_first * TOP_K] == experts[:, None], axis=1, dtype=jnp.int32)
    nblk = (counts + tm - 1) // tm
    blk_end = jnp.cumsum(nblk)
    blk_start = blk_end - nblk
    blk = jnp.arange(nb, dtype=jnp.int32)
    n_used = blk_end[-1]
    be = jnp.minimum(jnp.sum(blk_end[None, :] <= blk[:, None], axis=1, dtype=jnp.int32), n_experts - 1)
    mine = be[:, None] == experts[None, :]

    def of_expert(per_expert):
        return jnp.sum(jnp.where(mine, per_expert[None, :], 0), axis=1, dtype=jnp.int32)

    used = blk < n_used
    first_row = (blk - of_expert(blk_start)) * tm
    nv = jnp.where(used, jnp.clip(of_expert(counts) - first_row, 0, tm), 0).astype(jnp.int32)
    n_from_first = jnp.clip(of_expert(counts_first) - first_row, 0, nv).astype(jnp.int32)
    base = jnp.where(used, of_expert(starts) + first_row, 0).astype(jnp.int32)
    last_e = jnp.max(jnp.where(used, be, 0))
    be = jnp.where(used, be, last_e)
    return be, nv, n_from_first, base, order


def _combine_kernel(x1_ref, oa_ref, ob_ref, rt_ref, g2_ref, lng_ref, lnb_ref, x2_ref, *, alpha):
    y = oa_ref[...] * rt_ref[:, TOP_K:TOP_K + 1] + ob_ref[...] * rt_ref[:, TOP_K + 1:TOP_K + 2]
    t = alpha * x1_ref[...] + (1.0 + g2_ref[0]) * y
    x2_ref[...] = _layer_norm(t, lng_ref[...], lnb_ref[...])


def _combine(x1, out2, route, first_row, slot_stride, g2, ln_g, ln_b, *, per_token, ts, tiles_per_seq, alpha):
    n, d = x1.shape
    vec = pl.BlockSpec((1, d), lambda i: (0, 0))
    tile = pl.BlockSpec((ts, d), lambda i: (i, 0))
    off_a = first_row // ts
    off_b = (first_row + slot_stride) // ts
    return pl.pallas_call(
        functools.partial(_combine_kernel, alpha=alpha),
        grid=(n // ts,),
        in_specs=[tile,
                  pl.BlockSpec((ts, d), lambda i: (i + off_a, 0)),
                  pl.BlockSpec((ts, d), lambda i: (i + off_b, 0)),
                  pl.BlockSpec((ts, ROUTE_WIDTH), lambda i: (i, 0)),
                  _mod_spec(per_token, ts, tiles_per_seq, d), vec, vec],
        out_specs=tile,
        out_shape=jax.ShapeDtypeStruct((n, d), F32),
        compiler_params=_cparams(("parallel",)),
        name="moe_combine_ln",
    )(x1, out2, out2, route, g2, ln_g, ln_b)


def _split2(x):
    hi = x.astype(BF16)
    return hi, (x - hi.astype(F32)).astype(BF16)


def _stick_terms(z, vis, excl_ones):
    lb = _log_sigmoid(z)
    ls = lb - z
    if vis is not None:
        ls = jnp.where(vis, ls, 0.0)
    hi, lo = _split2(ls)
    later = jnp.dot(hi, excl_ones, preferred_element_type=F32) + jnp.dot(lo, excl_ones, preferred_element_type=F32)
    return lb, later, jnp.sum(ls, axis=-1, keepdims=True)


def _stick_weights(lb, later, run, vis):
    a = jnp.exp(lb + (later + run))
    if vis is not None:
        a = jnp.where(vis, a, 0.0)
    return a


def _stick_block(z, vis, excl_ones, run):
    lb, later, total = _stick_terms(z, vis, excl_ones)
    return _stick_weights(lb, later, run, vis), run + total


def _excl_ones(tk):
    j = lax.broadcasted_iota(jnp.int32, (tk, tk), 0)
    s = lax.broadcasted_iota(jnp.int32, (tk, tk), 1)
    return jnp.where(j > s, 1.0, 0.0).astype(BF16)


def _attn_kernel(bias_ref, q_ref, k_ref, v_ref, o_ref, *, t, dh, heads):
    scale = dh ** -0.5
    hg = pl.program_id(1)
    qi = pl.program_id(2)
    ones = _excl_ones(t)
    below = (lax.broadcasted_iota(jnp.int32, (t, t), 1) < lax.broadcasted_iota(jnp.int32, (t, t), 0))

    def blocks(starts, carry, masks):
        out = []
        for hh in range(heads):
            acc, run = carry[hh]
            lanes = slice(hh * dh, (hh + 1) * dh)
            terms = []
            for start, vis in zip(starts, masks):
                k = k_ref[0, pl.ds(start, t), lanes].astype(BF16)
                z = lax.dot_general(q_ref[0, :, lanes], k, (((1,), (1,)), ((), ())),
                                    preferred_element_type=F32) * scale + bias_ref[hg * heads + hh]
                terms.append(_stick_terms(z, vis, ones))
            for start, vis, (lb, later, total) in zip(starts, masks, terms):
                a = _stick_weights(lb, later, run, vis)
                run = run + total
                v = v_ref[0, pl.ds(start, t), lanes].astype(BF16)
                acc = acc + jnp.dot(a.astype(BF16), v, preferred_element_type=F32)
            out.append((acc, run))
        return tuple(out)

    init = tuple((jnp.zeros((t, dh), F32), jnp.zeros((t, 1), F32)) for _ in range(heads))
    diag = pl.multiple_of(qi * t, t)
    carry = lax.cond(qi == 0,
                     lambda: blocks([diag], init, [below]),
                     lambda: blocks([diag, pl.multiple_of(diag - t, t)], init, [below, None]))
    rest = jnp.maximum(qi - 1, 0)

    def pair(j, c):
        newer = pl.multiple_of((rest - 1 - 2 * j) * t, t)
        return blocks([newer, pl.multiple_of(newer - t, t)], c, [None, None])

    carry = lax.fori_loop(0, rest // 2, pair, carry)
    carry = lax.fori_loop(0, rest % 2, lambda j, c: blocks([0], c, [None]), carry)
    for hh in range(heads):
        o_ref[0, :, hh * dh:(hh + 1) * dh] = carry[hh][0].astype(o_ref.dtype)


def _attention_prompt(q, k, v, logit_bias, *, n_heads):
    b, s, hd = q.shape
    dh = hd // n_heads
    t = min(256, s)
    heads = next(h for h in (4, 2, 1) if n_heads % h == 0)
    w = heads * dh
    return pl.pallas_call(
        functools.partial(_attn_kernel, t=t, dh=dh, heads=heads),
        grid=(b, n_heads // heads, s // t),
        in_specs=[pl.BlockSpec(memory_space=pltpu.SMEM),
                  pl.BlockSpec((1, t, w), lambda bi, h, qi: (bi, qi, h)),
                  pl.BlockSpec((1, s, w), lambda bi, h, qi: (bi, 0, h)),
                  pl.BlockSpec((1, s, w), lambda bi, h, qi: (bi, 0, h))],
        out_specs=pl.BlockSpec((1, t, w), lambda bi, h, qi: (bi, qi, h)),
        out_shape=jax.ShapeDtypeStruct((b, s, hd), BF16),
        compiler_params=_cparams(("parallel", "parallel", "arbitrary")),
        name="stick_attention_prompt",
    )(logit_bias, q, k, v)


def _attn_paged_kernel(pt_ref, q_ref, bias_ref, own_ref, pick_ref, spread_ref, *rest, page, nq, pps):
    k_refs, v_refs = rest[:pps], rest[pps:2 * pps]
    kn_ref, vn_ref, o_ref, acc_ref, run_ref = rest[2 * pps:]
    step = pl.program_id(1)
    scale = q_ref.shape[-1] ** -0.5

    @pl.when(step == 0)
    def _():
        acc_ref[...] = jnp.zeros_like(acc_ref)
        run_ref[...] = jnp.zeros_like(run_ref)

    def process(pages, new_keys):
        own = own_ref[...]
        ones = _excl_ones(page)
        terms = []
        for k_ref, _ in pages:
            s_all = lax.dot_general(q_ref[0], k_ref[0].astype(BF16), (((1,), (1,)), ((), ())),
                                    preferred_element_type=F32)
            hi, lo = _split2(s_all * own)
            z = (jnp.dot(hi, pick_ref[...], preferred_element_type=F32)
                 + jnp.dot(lo, pick_ref[...], preferred_element_type=F32)) * scale + bias_ref[...]
            vis = None
            if new_keys:
                t = lax.broadcasted_iota(jnp.int32, z.shape, 0) % nq
                vis = lax.broadcasted_iota(jnp.int32, z.shape, 1) < t
            terms.append(_stick_terms(z, vis, ones) + (vis,))
        run = run_ref[...]
        acc = acc_ref[...]
        for (lb, later, total, vis), (_, v_ref) in zip(terms, pages):
            a = _stick_weights(lb, later, run, vis)
            run = run + total
            a_all = (jnp.dot(a.astype(BF16), spread_ref[...], preferred_element_type=F32) * own).astype(BF16)
            acc = acc + jnp.dot(a_all, v_ref[0].astype(BF16), preferred_element_type=F32)
        run_ref[...] = run
        acc_ref[...] = acc

    @pl.when(step == 0)
    def _():
        process([(kn_ref, vn_ref)], True)

    @pl.when(step > 0)
    def _():
        process(list(zip(k_refs, v_refs)), False)

    @pl.when(step == pl.num_programs(1) - 1)
    def _():
        o_ref[0] = acc_ref[...]


def _attention_paged(q_rows, bias_col, cache_k, cache_v, k_new, v_new, page_table, *, n_heads, nq):
    b, rows, dh = q_rows.shape
    n_pages = page_table.shape[1]
    cols = cache_k.shape[1]
    page = cols // n_heads
    col = jnp.arange(cols, dtype=jnp.int32)
    own = (col[None, :] % n_heads == jnp.arange(rows, dtype=jnp.int32)[:, None] // nq).astype(F32)
    pick = (col[:, None] // n_heads == jnp.arange(page, dtype=jnp.int32)[None, :]).astype(BF16)

    pps = next(p for p in (PAGES_PER_STEP, 2, 1) if n_pages % p == 0)

    def page_spec(j):
        return pl.BlockSpec(
            (1, cols, dh), lambda bi, s, pt: (pt[bi, n_pages - 1 - (jnp.maximum(s, 1) - 1) * pps - j], 0, 0))

    def const(shape):
        return pl.BlockSpec(shape, lambda bi, s, pt: (0,) * len(shape))

    grid_spec = pltpu.PrefetchScalarGridSpec(
        num_scalar_prefetch=1,
        grid=(b, n_pages // pps + 1),
        in_specs=[pl.BlockSpec((1, rows, dh), lambda bi, s, pt: (bi, 0, 0)),
                  const((rows, 1)), const((rows, cols)), const((cols, page)), const((page, cols))]
                 + [page_spec(j) for j in range(pps)] * 2
                 + [pl.BlockSpec((1, cols, dh), lambda bi, s, pt: (bi, 0, 0))] * 2,
        out_specs=pl.BlockSpec((1, rows, dh), lambda bi, s, pt: (bi, 0, 0)),
        scratch_shapes=[pltpu.VMEM((rows, dh), F32), pltpu.VMEM((rows, 1), F32)],
    )
    return pl.pallas_call(
        functools.partial(_attn_paged_kernel, page=page, nq=nq, pps=pps),
        grid_spec=grid_spec,
        out_shape=jax.ShapeDtypeStruct((b, rows, dh), F32),
        compiler_params=_cparams(("parallel", "arbitrary")),
        name="stick_attention_paged",
    )(page_table, q_rows, bias_col, own, pick, jnp.transpose(pick), *([cache_k] * pps), *([cache_v] * pps),
      k_new, v_new)


def kernel(x_prompt, x_sample, c_prompt, c_sample, cache_k, cache_v, page_table, ada_w, ada_b, ln1_g, ln1_b, ln2_g, ln2_b, a_w_in, a_b_in, a_vnorm_g, a_vnorm_b, a_w_s, a_b_s, a_w_out, kv_w, b_w_q, b_w_o, b_logit_bias, moe_router_group_w, moe_router_group_b, moe_router_expert_w, moe_router_expert_b, moe_w_gate, moe_w_up, moe_w_down):
    bp, sp, d = x_prompt.shape
    bs, ss, _ = x_sample.shape
    depth = ada_w.shape[0]
    n_a = a_w_in.shape[0]
    d_a = a_w_out.shape[1]
    chunk = a_w_s.shape[-1]
    n_pool, page, n_heads, dh = cache_k.shape
    hd = n_heads * dh
    n_groups, n_per = moe_router_expert_b.shape[1:]
    n_experts = n_groups * n_per
    alpha = (2 * depth) ** 0.25
    assert n_groups + n_experts <= ROUTE_WIDTH and ss <= SUBLANES and sp % chunk == 0
    np_tok, ns_tok = bp * sp, bs * ss
    ts_p = min(512, sp)
    tiles_p = sp // ts_p
    assert np_tok % ns_tok == 0

    c_rows = -(-(bp + bs) // SUBLANES) * SUBLANES
    c_all = jnp.concatenate([c_prompt, c_sample, jnp.zeros((c_rows - bp - bs, d), F32)], axis=0)
    mod = _modulation(c_all, ada_w, ada_b)

    def mods(l, group):
        m = mod[l].reshape(c_rows, 6, d)
        if group == 0:
            return [m[:bp, j].reshape(bp, 1, d) for j in range(6)]
        return [jnp.repeat(m[bp:bp + bs, j], ss, axis=0).reshape(1, ns_tok, d) for j in range(6)]

    groups = [dict(x=x_prompt.reshape(np_tok, d), ts=ts_p, tiles=tiles_p, per_token=False, n=np_tok),
              dict(x=x_sample.reshape(ns_tok, d), ts=ns_tok, tiles=1, per_token=True, n=ns_tok)]
    tm = 128
    slot_stride = -(-(np_tok + ns_tok) // ts_p) * ts_p
    assert slot_stride % ns_tok == 0 and (n_heads * ss) % SUBLANES == 0
    chunk_v = []
    k_out, v_out = [None, None], [None, None]

    for l in range(depth):
        w_route = jnp.concatenate(
            [moe_router_group_w[l],
             jnp.transpose(moe_router_expert_w[l], (1, 0, 2)).reshape(d, n_experts),
             jnp.zeros((d, ROUTE_WIDTH - n_groups - n_experts), F32)], axis=1).astype(BF16)
        b_route = jnp.concatenate(
            [moe_router_group_b[l], moe_router_expert_b[l].reshape(-1),
             jnp.zeros((ROUTE_WIDTH - n_groups - n_experts,), F32)]).reshape(1, ROUTE_WIDTH)
        x1s, h2s, routes, g2s = [], [], [], []
        for gi, grp in enumerate(groups):
            sh1, sc1, g1, sh2, sc2, g2 = mods(l, gi)
            x, ts, tiles, per_token = grp["x"], grp["ts"], grp["tiles"], grp["per_token"]
            if l < n_a:
                u, v_raw = _projection(x, a_w_in[l].astype(BF16), (0, d_a), d_a, (F32, F32), ts=ts,
                                       tiles_per_seq=tiles, mod=(sc1, sh1, per_token),
                                       bias=a_b_in[l].reshape(1, 2 * d_a), act=True, name="gmlp_in")
                bs_t = jnp.transpose(a_b_s[l])
                if gi == 0:
                    n_chunks = np_tok // chunk
                    gated, = _spatial_gate(v_raw.reshape(n_chunks, chunk, d_a), u.reshape(n_chunks, chunk, d_a),
                                           a_vnorm_g[l].reshape(1, d_a), a_vnorm_b[l].reshape(1, d_a),
                                           a_w_s[l], bs_t, emit_vn=False)
                    a_mix = gated.reshape(np_tok, d_a)
                else:
                    pad = ((0, 0), (0, SAMPLE_ROWS - ss), (0, 0))
                    gated, vn = _spatial_gate(jnp.pad(v_raw.reshape(bs, ss, d_a), pad),
                                              jnp.pad(u.reshape(bs, ss, d_a), pad),
                                              a_vnorm_g[l].reshape(1, d_a), a_vnorm_b[l].reshape(1, d_a),
                                              a_w_s[l][:, :SAMPLE_ROWS, :SAMPLE_ROWS], bs_t[:SAMPLE_ROWS], emit_vn=True)
                    a_mix = gated[:, :ss].reshape(ns_tok, d_a)
                    chunk_v.append(vn[:, :ss])
                w_mix = a_w_out[l].astype(BF16)
            else:
                j = l - n_a
                if k_out[gi] is None:
                    k_out[gi], v_out[gi] = _projection(x, kv_w.astype(BF16), (0, hd), hd, (F32, F32), ts=ts,
                                                       tiles_per_seq=tiles, name="kv_proj")
                q, = _projection(x, b_w_q[j].astype(BF16), (0,), hd, (BF16,), ts=ts, tiles_per_seq=tiles,
                                 mod=(sc1, sh1, per_token), name="q_proj")
                if gi == 0:
                    o = _attention_prompt(q.reshape(bp, sp, hd), k_out[gi].reshape(bp, sp, hd),
                                          v_out[gi].reshape(bp, sp, hd), b_logit_bias[j], n_heads=n_heads)
                    a_mix = o.reshape(np_tok, hd)
                else:
                    q_rows = jnp.transpose(q.reshape(bs, ss, n_heads, dh), (0, 2, 1, 3)).reshape(bs, n_heads * ss, dh)

                    def new_page(t):
                        t = jnp.pad(t.reshape(bs, ss, n_heads, dh), ((0, 0), (0, page - ss), (0, 0), (0, 0)))
                        return t.reshape(bs, page * n_heads, dh)

                    o = _attention_paged(q_rows, jnp.repeat(b_logit_bias[j], ss).reshape(n_heads * ss, 1),
                                         cache_k.reshape(n_pool, page * n_heads, dh),
                                         cache_v.reshape(n_pool, page * n_heads, dh),
                                         new_page(k_out[gi]), new_page(v_out[gi]), page_table,
                                         n_heads=n_heads, nq=ss)
                    o = jnp.transpose(o.reshape(bs, n_heads, ss, dh), (0, 2, 1, 3))
                    a_mix = o.reshape(ns_tok, hd).astype(BF16)
                w_mix = b_w_o[j].astype(BF16)
            x1, h2, route = _out_projection(a_mix, w_mix, x, g1, ln1_g[l].reshape(1, d), ln1_b[l].reshape(1, d),
                                            sc2, sh2, w_route, b_route, per_token=per_token, ts=ts,
                                            tiles_per_seq=tiles, alpha=alpha, n_groups=n_groups, n_per=n_per)
            x1s.append(x1)
            h2s.append(h2)
            routes.append(route)
            g2s.append(g2)

        be, nv, n_prompt_rows, base, order = _dispatch(jnp.concatenate(routes, axis=0), n_experts, tm, np_tok)
        out2 = _moe_experts(h2s[0], h2s[1], be, nv, n_prompt_rows, base, order, moe_w_gate, moe_w_up,
                            moe_w_down, l, tm=tm, slot_stride=slot_stride)
        first_rows = [0, np_tok]
        for gi, grp in enumerate(groups):
            grp["x"] = _combine(x1s[gi], out2, routes[gi], first_rows[gi], slot_stride, g2s[gi],
                                ln2_g[l].reshape(1, d), ln2_b[l].reshape(1, d), per_token=grp["per_token"],
                                ts=grp["ts"], tiles_per_seq=grp["tiles"], alpha=alpha)

    y_prompt = groups[0]["x"].reshape(bp, sp, d)
    y_sample = groups[1]["x"].reshape(bs, ss, d)
    return (y_prompt, y_sample,
            k_out[0].reshape(bp, sp, n_heads, dh), v_out[0].reshape(bp, sp, n_heads, dh),
            k_out[1].reshape(bs, ss, n_heads, dh), v_out[1].reshape(bs, ss, n_heads, dh),
            jnp.stack(chunk_v))
```

```python
import functools
import math

import jax
import jax.numpy as jnp
from jax import lax
from jax.experimental import pallas as pl
from jax.experimental.pallas import tpu as pltpu

F32 = jnp.float32
BF16 = jnp.bfloat16
LN_EPS = 1e-5
V7X_VMEM_LIMIT_BYTES = 56 * 1024 * 1024
LANES = 128
SUBLANES = 8
TOP_K = 2
ROUTE_WIDTH = LANES
EPILOGUE_ROWS = 128
DMA_UNROLL = 16
PAGES_PER_STEP = 4
SAMPLE_ROWS = 2 * SUBLANES


def _cparams(semantics):
    return pltpu.CompilerParams(dimension_semantics=semantics,
                                vmem_limit_bytes=V7X_VMEM_LIMIT_BYTES)


def _layer_norm(x, g, b):
    mu = jnp.mean(x, axis=-1, keepdims=True)
    xc = x - mu
    var = jnp.mean(xc * xc, axis=-1, keepdims=True)
    return xc * lax.rsqrt(var + LN_EPS) * g + b


def _gelu_tanh(x):
    c = math.sqrt(2.0 / math.pi)
    return x * (0.5 * (1.0 + jnp.tanh(c * (x + 0.044715 * (x * x * x)))))


def _log_sigmoid(z):
    return jnp.minimum(z, 0.0) - jnp.log(1.0 + jnp.exp(-jnp.abs(z)))


def _mod_spec(per_token, ts, tiles_per_seq, d):
    if per_token:
        return pl.BlockSpec((1, ts, d), lambda i, *_: (0, i, 0))
    return pl.BlockSpec((1, 1, d), lambda i, *_: (i // tiles_per_seq, 0, 0))


def _mod_kernel(c_ref, w_ref, b_ref, o_ref):
    c = c_ref[...].astype(BF16)
    w = w_ref[0].astype(BF16)
    o_ref[0] = jnp.dot(c, w, preferred_element_type=F32) + b_ref[0]


def _modulation(c_all, ada_w, ada_b):
    depth, d, d6 = ada_w.shape
    rows = c_all.shape[0]
    tn = 1024 if d6 % 1024 == 0 else d6
    return pl.pallas_call(
        _mod_kernel,
        grid=(depth, d6 // tn),
        in_specs=[pl.BlockSpec((rows, d), lambda l, n: (0, 0)),
                  pl.BlockSpec((1, d, tn), lambda l, n: (l, 0, n)),
                  pl.BlockSpec((1, 1, tn), lambda l, n: (l, 0, n))],
        out_specs=pl.BlockSpec((1, rows, tn), lambda l, n: (l, 0, n)),
        out_shape=jax.ShapeDtypeStruct((depth, rows, d6), F32),
        compiler_params=_cparams(("arbitrary", "arbitrary")),
        name="adaln_modulation",
    )(c_all, ada_w, ada_b.reshape(depth, 1, d6))


def _proj_kernel(*refs, n_out, modulate, has_bias, act, out_scale):
    refs = list(refs)
    x_ref = refs.pop(0)
    sc_ref = refs.pop(0) if modulate else None
    sh_ref = refs.pop(0) if modulate else None
    w_refs = [refs.pop(0) for _ in range(n_out)]
    b_refs = [refs.pop(0) for _ in range(n_out)] if has_bias else [None] * n_out
    o_refs = [refs.pop(0) for _ in range(n_out)]
    h_ref = refs.pop(0)

    @pl.when(pl.program_id(1) == 0)
    def _():
        x = x_ref[...]
        if modulate:
            x = x * (1.0 + sc_ref[0]) + sh_ref[0]
        h_ref[...] = x.astype(BF16)

    h = h_ref[...]
    for w_ref, b_ref, o_ref in zip(w_refs, b_refs, o_refs):
        z = jnp.dot(h, w_ref[...], preferred_element_type=F32)
        if has_bias:
            z = z + b_ref[...]
        if act:
            z = _gelu_tanh(z)
        if out_scale != 1.0:
            z = z * out_scale
        o_ref[...] = z.astype(o_ref.dtype)


def _projection(x, w, col_starts, width, out_dtypes, *, ts, tiles_per_seq, mod=None, bias=None,
                act=False, out_scale=1.0, name="projection"):
    n, d = x.shape
    tn = next(t for t in (1024, 512, width) if width % t == 0)
    n_out = len(col_starts)
    modulate = mod is not None
    in_specs = [pl.BlockSpec((ts, d), lambda i, j: (i, 0))]
    args = [x]
    if modulate:
        sc, sh, per_token = mod
        in_specs += [_mod_spec(per_token, ts, tiles_per_seq, d)] * 2
        args += [sc, sh]
    for c0 in col_starts:
        in_specs.append(pl.BlockSpec((d, tn), functools.partial(lambda i, j, o: (0, j + o), o=c0 // tn)))
        args.append(w)
    if bias is not None:
        for c0 in col_starts:
            in_specs.append(pl.BlockSpec((1, tn), functools.partial(lambda i, j, o: (0, j + o), o=c0 // tn)))
            args.append(bias)
    return pl.pallas_call(
        functools.partial(_proj_kernel, n_out=n_out, modulate=modulate, has_bias=bias is not None, act=act,
                          out_scale=out_scale),
        grid=(n // ts, width // tn),
        in_specs=in_specs,
        out_specs=[pl.BlockSpec((ts, tn), lambda i, j: (i, j))] * n_out,
        out_shape=[jax.ShapeDtypeStruct((n, width), dt) for dt in out_dtypes],
        scratch_shapes=[pltpu.VMEM((ts, d), BF16)],
        compiler_params=_cparams(("parallel", "arbitrary")),
        name=name,
    )(*args)


def _gate_kernel(v_ref, u_ref, g_ref, b_ref, ws_ref, bs_ref, o_ref, *maybe_vn_ref, n_groups):
    v = v_ref[0]
    vn = _layer_norm(v, g_ref[...], b_ref[...])
    if maybe_vn_ref:
        maybe_vn_ref[0][0] = vn
    vb = vn.astype(BF16)
    rows, d_a = v.shape
    gd = d_a // n_groups
    causal = (lax.broadcasted_iota(jnp.int32, (rows, rows), 0)
              >= lax.broadcasted_iota(jnp.int32, (rows, rows), 1))
    for g in range(n_groups):
        w = jnp.where(causal, ws_ref[g], 0.0).astype(BF16)
        s = jnp.dot(w, vb[:, g * gd:(g + 1) * gd], preferred_element_type=F32) + bs_ref[:, g:g + 1]
        o_ref[0, :, g * gd:(g + 1) * gd] = (u_ref[0, :, g * gd:(g + 1) * gd].astype(F32) * s).astype(o_ref.dtype)


def _spatial_gate(v_raw, u, vn_g, vn_b, w_s, b_s_t, *, emit_vn):
    c, r, d_a = v_raw.shape
    n_groups = w_s.shape[0]
    blk = pl.BlockSpec((1, r, d_a), lambda i: (i, 0, 0))
    out_shape = [jax.ShapeDtypeStruct((c, r, d_a), BF16)]
    out_specs = [blk]
    if emit_vn:
        out_shape.append(jax.ShapeDtypeStruct((c, r, d_a), F32))
        out_specs.append(blk)
    return pl.pallas_call(
        functools.partial(_gate_kernel, n_groups=n_groups),
        grid=(c,),
        in_specs=[blk, blk,
                  pl.BlockSpec((1, d_a), lambda i: (0, 0)),
                  pl.BlockSpec((1, d_a), lambda i: (0, 0)),
                  pl.BlockSpec((n_groups, r, r), lambda i: (0, 0, 0)),
                  pl.BlockSpec((r, n_groups), lambda i: (0, 0))],
        out_specs=out_specs,
        out_shape=out_shape,
        compiler_params=_cparams(("parallel",)),
        name="spatial_gate",
    )(v_raw, u, vn_g, vn_b, w_s, b_s_t)


def _route(logits, n_groups, n_per):
    lane = lax.broadcasted_iota(jnp.int32, logits.shape, 1)
    lane_f = lane.astype(F32)
    neg = -jnp.inf
    big = float(2 * ROUTE_WIDTH)
    gl = jnp.where(lane < n_groups, logits, neg)
    gmax = jnp.max(gl, axis=-1, keepdims=True)
    g_sel = jnp.min(jnp.where(gl == gmax, lane_f, big), axis=-1, keepdims=True)
    g_weight = 1.0 / jnp.sum(jnp.exp(gl - gmax), axis=-1, keepdims=True)
    e_lane = lane_f - float(n_groups)
    in_group = (e_lane >= g_sel * n_per) & (e_lane < (g_sel + 1.0) * n_per)
    el = jnp.where(in_group, logits, neg)
    v1 = jnp.max(el, axis=-1, keepdims=True)
    i1 = jnp.min(jnp.where(el == v1, lane_f, big), axis=-1, keepdims=True)
    el2 = jnp.where(lane_f == i1, neg, el)
    v2 = jnp.max(el2, axis=-1, keepdims=True)
    i2 = jnp.min(jnp.where(el2 == v2, lane_f, big), axis=-1, keepdims=True)
    e2 = jnp.exp(v2 - v1)
    gate1 = g_weight * (1.0 / (1.0 + e2))
    gate2 = g_weight * (e2 / (1.0 + e2))
    id1 = i1 - float(n_groups)
    id2 = i2 - float(n_groups)
    return jnp.where(lane == 0, id1,
                     jnp.where(lane == 1, id2,
                               jnp.where(lane == 2, gate1,
                                         jnp.where(lane == 3, gate2, 0.0))))


def _outproj_kernel(a_ref, w_ref, x_ref, g1_ref, lng_ref, lnb_ref, sc2_ref, sh2_ref, wr_ref, br_ref,
                    x1_ref, h2_ref, rt_ref, acc_ref, *, alpha, n_groups, n_per):
    k = pl.program_id(1)
    nk = pl.num_programs(1)
    ts = acc_ref.shape[0]

    def epilogue(r):
        g1 = g1_ref[0] if g1_ref.shape[1] == 1 else g1_ref[0, r, :]
        sc2 = sc2_ref[0] if sc2_ref.shape[1] == 1 else sc2_ref[0, r, :]
        sh2 = sh2_ref[0] if sh2_ref.shape[1] == 1 else sh2_ref[0, r, :]
        t = alpha * x_ref[r, :] + (1.0 + g1) * acc_ref[r, :]
        x1 = _layer_norm(t, lng_ref[...], lnb_ref[...])
        x1_ref[r, :] = x1
        h2 = x1 * (1.0 + sc2) + sh2
        h2_ref[r, :] = h2
        logits = jnp.dot(h2.astype(BF16), wr_ref[...], preferred_element_type=F32) + br_ref[...]
        rt_ref[r, :] = _route(logits, n_groups, n_per)

    @pl.when(k == 0)
    def _():
        acc_ref[...] = jnp.zeros_like(acc_ref)

    acc_ref[...] += jnp.dot(a_ref[...], w_ref[...], preferred_element_type=F32)

    @pl.when(k == nk - 1)
    def _():
        rows = min(EPILOGUE_ROWS, ts)

        def slab(s, c):
            epilogue(pl.ds(pl.multiple_of(s * rows, rows), rows))
            return c

        lax.fori_loop(0, ts // rows, slab, 0)


def _out_projection(a, w, x, g1, ln_g, ln_b, sc2, sh2, w_route, b_route, *, per_token, ts, tiles_per_seq,
                    alpha, n_groups, n_per):
    n, kdim = a.shape
    d = w.shape[1]
    tk = next(t for t in (1024, 512, kdim) if kdim % t == 0)
    mspec = _mod_spec(per_token, ts, tiles_per_seq, d)
    vec = pl.BlockSpec((1, d), lambda i, k: (0, 0))
    tile = pl.BlockSpec((ts, d), lambda i, k: (i, 0))
    return pl.pallas_call(
        functools.partial(_outproj_kernel, alpha=alpha, n_groups=n_groups, n_per=n_per),
        grid=(n // ts, kdim // tk),
        in_specs=[pl.BlockSpec((ts, tk), lambda i, k: (i, k)),
                  pl.BlockSpec((tk, d), lambda i, k: (k, 0)),
                  tile, mspec, vec, vec, mspec, mspec,
                  pl.BlockSpec((d, ROUTE_WIDTH), lambda i, k: (0, 0)),
                  pl.BlockSpec((1, ROUTE_WIDTH), lambda i, k: (0, 0))],
        out_specs=[tile, tile, pl.BlockSpec((ts, ROUTE_WIDTH), lambda i, k: (i, 0))],
        out_shape=[jax.ShapeDtypeStruct((n, d), F32), jax.ShapeDtypeStruct((n, d), F32),
                   jax.ShapeDtypeStruct((n, ROUTE_WIDTH), F32)],
        scratch_shapes=[pltpu.VMEM((ts, d), F32)],
        compiler_params=_cparams(("parallel", "arbitrary")),
        name="mixer_out_ln_route",
    )(a, w, x, g1, ln_g, ln_b, sc2, sh2, w_route, b_route)


def _moe_kernel(be_ref, nv_ref, na_ref, base_ref, ord_ref, ha_hbm, hb_hbm, wg_ref, wu_ref, wd_ref, out_hbm,
                xbuf, obuf, wgb, wub, wdb, gsem, ssem, *, tm, slot_stride, n_rows):
    i = pl.program_id(0)
    nb = pl.num_programs(0)
    slot = i % 2
    n_a = ha_hbm.shape[0]

    def gather_copy(blk, s, r, from_b):
        tok = ord_ref[base_ref[blk] + r] >> 1
        src = hb_hbm.at[pl.ds(tok - n_a, 1)] if from_b else ha_hbm.at[pl.ds(tok, 1)]
        return pltpu.make_async_copy(src, xbuf.at[s, pl.ds(r, 1)], gsem.at[s])

    def scatter_copy(blk, s, r):
        a = ord_ref[base_ref[blk] + r]
        dst = (a & 1) * slot_stride + (a >> 1)
        return pltpu.make_async_copy(obuf.at[s, pl.ds(r, 1)], out_hbm.at[pl.ds(dst, 1)], ssem.at[s])

    def for_rows(n, fn):
        n_groups = n // DMA_UNROLL

        def group(gidx, c):
            for u in range(DMA_UNROLL):
                fn(gidx * DMA_UNROLL + u)
            return c

        def tail(r, c):
            fn(r)
            return c

        lax.fori_loop(0, n_groups, group, 0)
        lax.fori_loop(n_groups * DMA_UNROLL, n, tail, 0)

    def start_gather(blk, s):
        n_from_a = na_ref[blk]
        for_rows(n_from_a, lambda r: gather_copy(blk, s, r, False).start())

        def from_b(r, c):
            gather_copy(blk, s, r, True).start()
            return c

        lax.fori_loop(n_from_a, nv_ref[blk], from_b, 0)

    def wait_rows(n, s, gather):
        def copy(rows):
            if gather:
                return pltpu.make_async_copy(ha_hbm.at[rows], xbuf.at[s, rows], gsem.at[s])
            return pltpu.make_async_copy(obuf.at[s, rows], out_hbm.at[rows], ssem.at[s])

        n_aligned = pl.multiple_of((n // SUBLANES) * SUBLANES, SUBLANES)

        @pl.when(n_aligned > 0)
        def _():
            copy(pl.ds(0, n_aligned)).wait()

        def tail(r, c):
            copy(pl.ds(0, 1)).wait()
            return c

        lax.fori_loop(n_aligned, n, tail, 0)

    prev = jnp.maximum(i - 1, 0)

    @pl.when(i == 0)
    def _():
        obuf[...] = jnp.zeros_like(obuf)
        for half in range(TOP_K):
            for start in range(n_rows, slot_stride, tm):
                size = min(tm, slot_stride - start)
                pltpu.sync_copy(obuf.at[0, pl.ds(0, size)], out_hbm.at[pl.ds(half * slot_stride + start, size)])
        xbuf[...] = jnp.zeros_like(xbuf)
        start_gather(0, 0)

    @pl.when(i + 1 < nb)
    def _():
        start_gather(jnp.minimum(i + 1, nb - 1), 1 - slot)

    @pl.when(i >= 2)
    def _():
        wait_rows(nv_ref[jnp.maximum(i - 2, 0)], slot, False)

    @pl.when(nv_ref[i] > 0)
    def _():
        wait_rows(nv_ref[i], slot, True)

        @pl.when((i == 0) | (be_ref[i] != be_ref[prev]))
        def _():
            wgb[...] = wg_ref[0, 0].astype(BF16)
            wub[...] = wu_ref[0, 0].astype(BF16)
            wdb[...] = wd_ref[0, 0].astype(BF16)

        x = xbuf[slot].astype(BF16)
        g = jnp.dot(x, wgb[...], preferred_element_type=F32)
        u = jnp.dot(x, wub[...], preferred_element_type=F32)
        mid = (g * jax.nn.sigmoid(g) * u).astype(BF16)
        o = jnp.dot(mid, wdb[...], preferred_element_type=F32)
        obuf[slot] = o
        for_rows(nv_ref[i], lambda r: scatter_copy(i, slot, r).start())

    @pl.when(i == nb - 1)
    def _():
        @pl.when(i >= 1)
        def _():
            wait_rows(nv_ref[prev], 1 - slot, False)
        wait_rows(nv_ref[i], slot, False)


def _moe_experts(h2_a, h2_b, blk_expert, blk_nvalid, blk_from_a, blk_base, order, w_gate, w_up, w_down, layer,
                 *, tm, slot_stride):
    d = h2_a.shape[1]
    n = h2_a.shape[0] + h2_b.shape[0]
    _, n_experts, _, de = w_gate.shape
    nb = blk_expert.shape[0]

    def expert_block(i, be, nv, na, base, order):
        return (layer, be[i], 0, 0)

    grid_spec = pltpu.PrefetchScalarGridSpec(
        num_scalar_prefetch=5,
        grid=(nb,),
        in_specs=[pl.BlockSpec(memory_space=pl.ANY), pl.BlockSpec(memory_space=pl.ANY),
                  pl.BlockSpec((1, 1, d, de), expert_block),
                  pl.BlockSpec((1, 1, d, de), expert_block),
                  pl.BlockSpec((1, 1, de, d), expert_block)],
        out_specs=pl.BlockSpec(memory_space=pl.ANY),
        scratch_shapes=[pltpu.VMEM((2, tm, d), F32), pltpu.VMEM((2, tm, d), F32),
                        pltpu.VMEM((d, de), BF16), pltpu.VMEM((d, de), BF16), pltpu.VMEM((de, d), BF16),
                        pltpu.SemaphoreType.DMA((2,)), pltpu.SemaphoreType.DMA((2,))],
    )
    return pl.pallas_call(
        functools.partial(_moe_kernel, tm=tm, slot_stride=slot_stride, n_rows=n),
        grid_spec=grid_spec,
        out_shape=jax.ShapeDtypeStruct((TOP_K * slot_stride, d), F32),
        compiler_params=_cparams(("arbitrary",)),
        name="moe_experts",
    )(blk_expert, blk_nvalid, blk_from_a, blk_base, order, h2_a, h2_b, w_gate, w_up, w_down)


def _dispatch(route, n_experts, tm, n_first):
    n = route.shape[0]
    eid = route[:, :TOP_K].astype(jnp.int32).reshape(-1)
    n_assign = n * TOP_K
    nb = -(-n_assign // tm) + n_experts
    _, order = lax.sort((eid, jnp.arange(n_assign, dtype=jnp.int32)), num_keys=1)
    experts = jnp.arange(n_experts, dtype=jnp.int32)
    starts = jnp.sum(eid[None, :] < experts[:, None], axis=1, dtype=jnp.int32)
    counts = jnp.sum(eid[None, :] == experts[:, None], axis=1, dtype=jnp.int32)
    counts_first = jnp.sum(eid[None, :n_first * TOP_K] == experts[:, None], axis=1, dtype=jnp.int32)
    nblk = (counts + tm - 1) // tm
    blk_end = jnp.cumsum(nblk)
    blk_start = blk_end - nblk
    blk = jnp.arange(nb, dtype=jnp.int32)
    n_used = blk_end[-1]
    be = jnp.minimum(jnp.sum(blk_end[None, :] <= blk[:, None], axis=1, dtype=jnp.int32), n_experts - 1)
    mine = be[:, None] == experts[None, :]

    def of_expert(per_expert):
        return jnp.sum(jnp.where(mine, per_expert[None, :], 0), axis=1, dtype=jnp.int32)

    used = blk < n_used
    first_row = (blk - of_expert(blk_start)) * tm
    nv = jnp.where(used, jnp.clip(of_expert(counts) - first_row, 0, tm), 0).astype(jnp.int32)
    n_from_first = jnp.clip(of_expert(counts_first) - first_row, 0, nv).astype(jnp.int32)
    base = jnp.where(used, of_expert(starts) + first_row, 0).astype(jnp.int32)
    last_e = jnp.max(jnp.where(used, be, 0))
    be = jnp.where(used, be, last_e)
    return be, nv, n_from_first, base, order


def _combine_kernel(x1_ref, oa_ref, ob_ref, rt_ref, g2_ref, lng_ref, lnb_ref, x2_ref, *, alpha):
    y = oa_ref[...] * rt_ref[:, TOP_K:TOP_K + 1] + ob_ref[...] * rt_ref[:, TOP_K + 1:TOP_K + 2]
    t = alpha * x1_ref[...] + (1.0 + g2_ref[0]) * y
    x2_ref[...] = _layer_norm(t, lng_ref[...], lnb_ref[...])


def _combine(x1, out2, route, first_row, slot_stride, g2, ln_g, ln_b, *, per_token, ts, tiles_per_seq, alpha):
    n, d = x1.shape
    vec = pl.BlockSpec((1, d), lambda i: (0, 0))
    tile = pl.BlockSpec((ts, d), lambda i: (i, 0))
    off_a = first_row // ts
    off_b = (first_row + slot_stride) // ts
    return pl.pallas_call(
        functools.partial(_combine_kernel, alpha=alpha),
        grid=(n // ts,),
        in_specs=[tile,
                  pl.BlockSpec((ts, d), lambda i: (i + off_a, 0)),
                  pl.BlockSpec((ts, d), lambda i: (i + off_b, 0)),
                  pl.BlockSpec((ts, ROUTE_WIDTH), lambda i: (i, 0)),
                  _mod_spec(per_token, ts, tiles_per_seq, d), vec, vec],
        out_specs=tile,
        out_shape=jax.ShapeDtypeStruct((n, d), F32),
        compiler_params=_cparams(("parallel",)),
        name="moe_combine_ln",
    )(x1, out2, out2, route, g2, ln_g, ln_b)


def _split2(x):
    hi = x.astype(BF16)
    return hi, (x - hi.astype(F32)).astype(BF16)


def _stick_terms(z, vis, excl_ones):
    lb = _log_sigmoid(z)
    ls = lb - z
    if vis is not None:
        ls = jnp.where(vis, ls, 0.0)
    hi, lo = _split2(ls)
    later = jnp.dot(hi, excl_ones, preferred_element_type=F32) + jnp.dot(lo, excl_ones, preferred_element_type=F32)
    return lb, later, jnp.sum(ls, axis=-1, keepdims=True)


def _stick_weights(lb, later, run, vis):
    a = jnp.exp(lb + (later + run))
    if vis is not None:
        a = jnp.where(vis, a, 0.0)
    return a


def _stick_block(z, vis, excl_ones, run):
    lb, later, total = _stick_terms(z, vis, excl_ones)
    return _stick_weights(lb, later, run, vis), run + total


def _excl_ones(tk):
    j = lax.broadcasted_iota(jnp.int32, (tk, tk), 0)
    s = lax.broadcasted_iota(jnp.int32, (tk, tk), 1)
    return jnp.where(j > s, 1.0, 0.0).astype(BF16)


def _attn_kernel(bias_ref, q_ref, k_ref, v_ref, o_ref, *, t, dh, heads):
    scale = dh ** -0.5
    hg = pl.program_id(1)
    qi = pl.program_id(2)
    ones = _excl_ones(t)
    below = (lax.broadcasted_iota(jnp.int32, (t, t), 1) < lax.broadcasted_iota(jnp.int32, (t, t), 0))

    def blocks(starts, carry, masks):
        out = []
        for hh in range(heads):
            acc, run = carry[hh]
            lanes = slice(hh * dh, (hh + 1) * dh)
            terms = []
            for start, vis in zip(starts, masks):
                k = k_ref[0, pl.ds(start, t), lanes].astype(BF16)
                z = lax.dot_general(q_ref[0, :, lanes], k, (((1,), (1,)), ((), ())),
                                    preferred_element_type=F32) * scale + bias_ref[hg * heads + hh]
                terms.append(_stick_terms(z, vis, ones))
            for start, vis, (lb, later, total) in zip(starts, masks, terms):
                a = _stick_weights(lb, later, run, vis)
                run = run + total
                v = v_ref[0, pl.ds(start, t), lanes].astype(BF16)
                acc = acc + jnp.dot(a.astype(BF16), v, preferred_element_type=F32)
            out.append((acc, run))
        return tuple(out)

    init = tuple((jnp.zeros((t, dh), F32), jnp.zeros((t, 1), F32)) for _ in range(heads))
    diag = pl.multiple_of(qi * t, t)
    carry = lax.cond(qi == 0,
                     lambda: blocks([diag], init, [below]),
                     lambda: blocks([diag, pl.multiple_of(diag - t, t)], init, [below, None]))
    rest = jnp.maximum(qi - 1, 0)

    def pair(j, c):
        newer = pl.multiple_of((rest - 1 - 2 * j) * t, t)
        return blocks([newer, pl.multiple_of(newer - t, t)], c, [None, None])

    carry = lax.fori_loop(0, rest // 2, pair, carry)
    carry = lax.fori_loop(0, rest % 2, lambda j, c: blocks([0], c, [None]), carry)
    for hh in range(heads):
        o_ref[0, :, hh * dh:(hh + 1) * dh] = carry[hh][0].astype(o_ref.dtype)


def _attention_prompt(q, k, v, logit_bias, *, n_heads):
    b, s, hd = q.shape
    dh = hd // n_heads
    t = min(256, s)
    heads = next(h for h in (4, 2, 1) if n_heads % h == 0)
    w = heads * dh
    return pl.pallas_call(
        functools.partial(_attn_kernel, t=t, dh=dh, heads=heads),
        grid=(b, n_heads // heads, s // t),
        in_specs=[pl.BlockSpec(memory_space=pltpu.SMEM),
                  pl.BlockSpec((1, t, w), lambda bi, h, qi: (bi, qi, h)),
                  pl.BlockSpec((1, s, w), lambda bi, h, qi: (bi, 0, h)),
                  pl.BlockSpec((1, s, w), lambda bi, h, qi: (bi, 0, h))],
        out_specs=pl.BlockSpec((1, t, w), lambda bi, h, qi: (bi, qi, h)),
        out_shape=jax.ShapeDtypeStruct((b, s, hd), BF16),
        compiler_params=_cparams(("parallel", "parallel", "arbitrary")),
        name="stick_attention_prompt",
    )(logit_bias, q, k, v)


def _attn_paged_kernel(pt_ref, q_ref, bias_ref, own_ref, pick_ref, spread_ref, *rest, page, nq, pps):
    k_refs, v_refs = rest[:pps], rest[pps:2 * pps]
    kn_ref, vn_ref, o_ref, acc_ref, run_ref = rest[2 * pps:]
    step = pl.program_id(1)
    scale = q_ref.shape[-1] ** -0.5

    @pl.when(step == 0)
    def _():
        acc_ref[...] = jnp.zeros_like(acc_ref)
        run_ref[...] = jnp.zeros_like(run_ref)

    def process(pages, new_keys):
        own = own_ref[...]
        ones = _excl_ones(page)
        terms = []
        for k_ref, _ in pages:
            s_all = lax.dot_general(q_ref[0], k_ref[0].astype(BF16), (((1,), (1,)), ((), ())),
                                    preferred_element_type=F32)
            hi, lo = _split2(s_all * own)
            z = (jnp.dot(hi, pick_ref[...], preferred_element_type=F32)
                 + jnp.dot(lo, pick_ref[...], preferred_element_type=F32)) * scale + bias_ref[...]
            vis = None
            if new_keys:
                t = lax.broadcasted_iota(jnp.int32, z.shape, 0) % nq
                vis = lax.broadcasted_iota(jnp.int32, z.shape, 1) < t
            terms.append(_stick_terms(z, vis, ones) + (vis,))
        run = run_ref[...]
        acc = acc_ref[...]
        for (lb, later, total, vis), (_, v_ref) in zip(terms, pages):
            a = _stick_weights(lb, later, run, vis)
            run = run + total
            a_all = (jnp.dot(a.astype(BF16), spread_ref[...], preferred_element_type=F32) * own).astype(BF16)
            acc = acc + jnp.dot(a_all, v_ref[0].astype(BF16), preferred_element_type=F32)
        run_ref[...] = run
        acc_ref[...] = acc

    @pl.when(step == 0)
    def _():
        process([(kn_ref, vn_ref)], True)

    @pl.when(step > 0)
    def _():
        process(list(zip(k_refs, v_refs)), False)

    @pl.when(step == pl.num_programs(1) - 1)
    def _():
        o_ref[0] = acc_ref[...]


def _attention_paged(q_rows, bias_col, cache_k, cache_v, k_new, v_new, page_table, *, n_heads, nq):
    b, rows, dh = q_rows.shape
    n_pages = page_table.shape[1]
    cols = cache_k.shape[1]
    page = cols // n_heads
    col = jnp.arange(cols, dtype=jnp.int32)
    own = (col[None, :] % n_heads == jnp.arange(rows, dtype=jnp.int32)[:, None] // nq).astype(F32)
    pick = (col[:, None] // n_heads == jnp.arange(page, dtype=jnp.int32)[None, :]).astype(BF16)

    pps = next(p for p in (PAGES_PER_STEP, 2, 1) if n_pages % p == 0)

    def page_spec(j):
        return pl.BlockSpec(
            (1, cols, dh), lambda bi, s, pt: (pt[bi, n_pages - 1 - (jnp.maximum(s, 1) - 1) * pps - j], 0, 0))

    def const(shape):
        return pl.BlockSpec(shape, lambda bi, s, pt: (0,) * len(shape))

    grid_spec = pltpu.PrefetchScalarGridSpec(
        num_scalar_prefetch=1,
        grid=(b, n_pages // pps + 1),
        in_specs=[pl.BlockSpec((1, rows, dh), lambda bi, s, pt: (bi, 0, 0)),
                  const((rows, 1)), const((rows, cols)), const((cols, page)), const((page, cols))]
                 + [page_spec(j) for j in range(pps)] * 2
                 + [pl.BlockSpec((1, cols, dh), lambda bi, s, pt: (bi, 0, 0))] * 2,
        out_specs=pl.BlockSpec((1, rows, dh), lambda bi, s, pt: (bi, 0, 0)),
        scratch_shapes=[pltpu.VMEM((rows, dh), F32), pltpu.VMEM((rows, 1), F32)],
    )
    return pl.pallas_call(
        functools.partial(_attn_paged_kernel, page=page, nq=nq, pps=pps),
        grid_spec=grid_spec,
        out_shape=jax.ShapeDtypeStruct((b, rows, dh), F32),
        compiler_params=_cparams(("parallel", "arbitrary")),
        name="stick_attention_paged",
    )(page_table, q_rows, bias_col, own, pick, jnp.transpose(pick), *([cache_k] * pps), *([cache_v] * pps),
      k_new, v_new)


def kernel(x_prompt, x_sample, c_prompt, c_sample, cache_k, cache_v, page_table, ada_w, ada_b, ln1_g, ln1_b, ln2_g, ln2_b, a_w_in, a_b_in, a_vnorm_g, a_vnorm_b, a_w_s, a_b_s, a_w_out, kv_w, b_w_q, b_w_o, b_logit_bias, moe_router_group_w, moe_router_group_b, moe_router_expert_w, moe_router_expert_b, moe_w_gate, moe_w_up, moe_w_down):
    bp, sp, d = x_prompt.shape
    bs, ss, _ = x_sample.shape
    depth = ada_w.shape[0]
    n_a = a_w_in.shape[0]
    d_a = a_w_out.shape[1]
    chunk = a_w_s.shape[-1]
    n_pool, page, n_heads, dh = cache_k.shape
    hd = n_heads * dh
    n_groups, n_per = moe_router_expert_b.shape[1:]
    n_experts = n_groups * n_per
    alpha = (2 * depth) ** 0.25
    assert n_groups + n_experts <= ROUTE_WIDTH and ss <= SUBLANES and sp % chunk == 0
    np_tok, ns_tok = bp * sp, bs * ss
    ts_p = min(512, sp)
    tiles_p = sp // ts_p
    assert np_tok % ns_tok == 0

    c_rows = -(-(bp + bs) // SUBLANES) * SUBLANES
    c_all = jnp.concatenate([c_prompt, c_sample, jnp.zeros((c_rows - bp - bs, d), F32)], axis=0)
    mod = _modulation(c_all, ada_w, ada_b)

    def mods(l, group):
        m = mod[l].reshape(c_rows, 6, d)
        if group == 0:
            return [m[:bp, j].reshape(bp, 1, d) for j in range(6)]
        return [jnp.repeat(m[bp:bp + bs, j], ss, axis=0).reshape(1, ns_tok, d) for j in range(6)]

    groups = [dict(x=x_prompt.reshape(np_tok, d), ts=ts_p, tiles=tiles_p, per_token=False, n=np_tok),
              dict(x=x_sample.reshape(ns_tok, d), ts=ns_tok, tiles=1, per_token=True, n=ns_tok)]
    tm = 128
    slot_stride = -(-(np_tok + ns_tok) // ts_p) * ts_p
    assert slot_stride % ns_tok == 0 and (n_heads * ss) % SUBLANES == 0
    chunk_v = []
    k_out, v_out = [None, None], [None, None]

    for l in range(depth):
        w_route = jnp.concatenate(
            [moe_router_group_w[l],
             jnp.transpose(moe_router_expert_w[l], (1, 0, 2)).reshape(d, n_experts),
             jnp.zeros((d, ROUTE_WIDTH - n_groups - n_experts), F32)], axis=1).astype(BF16)
        b_route = jnp.concatenate(
            [moe_router_group_b[l], moe_router_expert_b[l].reshape(-1),
             jnp.zeros((ROUTE_WIDTH - n_groups - n_experts,), F32)]).reshape(1, ROUTE_WIDTH)
        x1s, h2s, routes, g2s = [], [], [], []
        for gi, grp in enumerate(groups):
            sh1, sc1, g1, sh2, sc2, g2 = mods(l, gi)
            x, ts, tiles, per_token = grp["x"], grp["ts"], grp["tiles"], grp["per_token"]
            if l < n_a:
                u, v_raw = _projection(x, a_w_in[l].astype(BF16), (0, d_a), d_a, (F32, F32), ts=ts,
                                       tiles_per_seq=tiles, mod=(sc1, sh1, per_token),
                                       bias=a_b_in[l].reshape(1, 2 * d_a), act=True, name="gmlp_in")
                bs_t = jnp.transpose(a_b_s[l])
                if gi == 0:
                    n_chunks = np_tok // chunk
                    gated, = _spatial_gate(v_raw.reshape(n_chunks, chunk, d_a), u.reshape(n_chunks, chunk, d_a),
                                           a_vnorm_g[l].reshape(1, d_a), a_vnorm_b[l].reshape(1, d_a),
                                           a_w_s[l], bs_t, emit_vn=False)
                    a_mix = gated.reshape(np_tok, d_a)
                else:
                    pad = ((0, 0), (0, SAMPLE_ROWS - ss), (0, 0))
                    gated, vn = _spatial_gate(jnp.pad(v_raw.reshape(bs, ss, d_a), pad),
                                              jnp.pad(u.reshape(bs, ss, d_a), pad),
                                              a_vnorm_g[l].reshape(1, d_a), a_vnorm_b[l].reshape(1, d_a),
                                              a_w_s[l][:, :SAMPLE_ROWS, :SAMPLE_ROWS], bs_t[:SAMPLE_ROWS], emit_vn=True)
                    a_mix = gated[:, :ss].reshape(ns_tok, d_a)
                    chunk_v.append(vn[:, :ss])
                w_mix = a_w_out[l].astype(BF16)
            else:
                j = l - n_a
                if k_out[gi] is None:
                    k_out[gi], v_out[gi] = _projection(x, kv_w.astype(BF16), (0, hd), hd, (F32, F32), ts=ts,
                                                       tiles_per_seq=tiles, name="kv_proj")
                q, = _projection(x, b_w_q[j].astype(BF16), (0,), hd, (BF16,), ts=ts, tiles_per_seq=tiles,
                                 mod=(sc1, sh1, per_token), name="q_proj")
                if gi == 0:
                    o = _attention_prompt(q.reshape(bp, sp, hd), k_out[gi].reshape(bp, sp, hd),
                                          v_out[gi].reshape(bp, sp, hd), b_logit_bias[j], n_heads=n_heads)
                    a_mix = o.reshape(np_tok, hd)
                else:
                    q_rows = jnp.transpose(q.reshape(bs, ss, n_heads, dh), (0, 2, 1, 3)).reshape(bs, n_heads * ss, dh)

                    def new_page(t):
                        t = jnp.pad(t.reshape(bs, ss, n_heads, dh), ((0, 0), (0, page - ss), (0, 0), (0, 0)))
                        return t.reshape(bs, page * n_heads, dh)

                    o = _attention_paged(q_rows, jnp.repeat(b_logit_bias[j], ss).reshape(n_heads * ss, 1),
                                         cache_k.reshape(n_pool, page * n_heads, dh),
                                         cache_v.reshape(n_pool, page * n_heads, dh),
                                         new_page(k_out[gi]), new_page(v_out[gi]), page_table,
                                         n_heads=n_heads, nq=ss)
                    o = jnp.transpose(o.reshape(bs, n_heads, ss, dh), (0, 2, 1, 3))
                    a_mix = o.reshape(ns_tok, hd).astype(BF16)
                w_mix = b_w_o[j].astype(BF16)
            x1, h2, route = _out_projection(a_mix, w_mix, x, g1, ln1_g[l].reshape(1, d), ln1_b[l].reshape(1, d),
                                            sc2, sh2, w_route, b_route, per_token=per_token, ts=ts,
                                            tiles_per_seq=tiles, alpha=alpha, n_groups=n_groups, n_per=n_per)
            x1s.append(x1)
            h2s.append(h2)
            routes.append(route)
            g2s.append(g2)

        be, nv, n_prompt_rows, base, order = _dispatch(jnp.concatenate(routes, axis=0), n_experts, tm, np_tok)
        out2 = _moe_experts(h2s[0], h2s[1], be, nv, n_prompt_rows, base, order, moe_w_gate, moe_w_up,
                            moe_w_down, l, tm=tm, slot_stride=slot_stride)
        first_rows = [0, np_tok]
        for gi, grp in enumerate(groups):
            grp["x"] = _combine(x1s[gi], out2, routes[gi], first_rows[gi], slot_stride, g2s[gi],
                                ln2_g[l].reshape(1, d), ln2_b[l].reshape(1, d), per_token=grp["per_token"],
                                ts=grp["ts"], tiles_per_seq=grp["tiles"], alpha=alpha)

    y_prompt = groups[0]["x"].reshape(bp, sp, d)
    y_sample = groups[1]["x"].reshape(bs, ss, d)
    return (y_prompt, y_sample,
            k_out[0].reshape(bp, sp, n_heads, dh), v_out[0].reshape(bp, sp, n_heads, dh),
            k_out[1].reshape(bs, ss, n_heads, dh), v_out[1].reshape(bs, ss, n_heads, dh),
            jnp.stack(chunk_v))
```

```python
import functools
import math

import jax
import jax.numpy as jnp
from jax import lax
from jax.experimental import pallas as pl
from jax.experimental.pallas import tpu as pltpu

F32 = jnp.float32
BF16 = jnp.bfloat16
LN_EPS = 1e-5
V7X_VMEM_LIMIT_BYTES = 56 * 1024 * 1024
LANES = 128
SUBLANES = 8
TOP_K = 2
ROUTE_WIDTH = LANES
EPILOGUE_ROWS = 128
DMA_UNROLL = 16
PAGES_PER_STEP = 4
SAMPLE_ROWS = 2 * SUBLANES


def _cparams(semantics):
    return pltpu.CompilerParams(dimension_semantics=semantics,
                                vmem_limit_bytes=V7X_VMEM_LIMIT_BYTES)


def _layer_norm(x, g, b):
    mu = jnp.mean(x, axis=-1, keepdims=True)
    xc = x - mu
    var = jnp.mean(xc * xc, axis=-1, keepdims=True)
    return xc * lax.rsqrt(var + LN_EPS) * g + b


def _gelu_tanh(x):
    c = math.sqrt(2.0 / math.pi)
    return x * (0.5 * (1.0 + jnp.tanh(c * (x + 0.044715 * (x * x * x)))))


def _log_sigmoid(z):
    return jnp.minimum(z, 0.0) - jnp.log(1.0 + jnp.exp(-jnp.abs(z)))


def _mod_spec(per_token, ts, tiles_per_seq, d):
    if per_token:
        return pl.BlockSpec((1, ts, d), lambda i, *_: (0, i, 0))
    return pl.BlockSpec((1, 1, d), lambda i, *_: (i // tiles_per_seq, 0, 0))


def _mod_kernel(c_ref, w_ref, b_ref, o_ref):
    c = c_ref[...].astype(BF16)
    w = w_ref[0].astype(BF16)
    o_ref[0] = jnp.dot(c, w, preferred_element_type=F32) + b_ref[0]


def _modulation(c_all, ada_w, ada_b):
    depth, d, d6 = ada_w.shape
    rows = c_all.shape[0]
    tn = 1024 if d6 % 1024 == 0 else d6
    return pl.pallas_call(
        _mod_kernel,
        grid=(depth, d6 // tn),
        in_specs=[pl.BlockSpec((rows, d), lambda l, n: (0, 0)),
                  pl.BlockSpec((1, d, tn), lambda l, n: (l, 0, n)),
                  pl.BlockSpec((1, 1, tn), lambda l, n: (l, 0, n))],
        out_specs=pl.BlockSpec((1, rows, tn), lambda l, n: (l, 0, n)),
        out_shape=jax.ShapeDtypeStruct((depth, rows, d6), F32),
        compiler_params=_cparams(("arbitrary", "arbitrary")),
        name="adaln_modulation",
    )(c_all, ada_w, ada_b.reshape(depth, 1, d6))


def _proj_kernel(*refs, n_out, modulate, has_bias, act, out_scale):
    refs = list(refs)
    x_ref = refs.pop(0)
    sc_ref = refs.pop(0) if modulate else None
    sh_ref = refs.pop(0) if modulate else None
    w_refs = [refs.pop(0) for _ in range(n_out)]
    b_refs = [refs.pop(0) for _ in range(n_out)] if has_bias else [None] * n_out
    o_refs = [refs.pop(0) for _ in range(n_out)]
    h_ref = refs.pop(0)

    @pl.when(pl.program_id(1) == 0)
    def _():
        x = x_ref[...]
        if modulate:
            x = x * (1.0 + sc_ref[0]) + sh_ref[0]
        h_ref[...] = x.astype(BF16)

    h = h_ref[...]
    for w_ref, b_ref, o_ref in zip(w_refs, b_refs, o_refs):
        z = jnp.dot(h, w_ref[...], preferred_element_type=F32)
        if has_bias:
            z = z + b_ref[...]
        if act:
            z = _gelu_tanh(z)
        if out_scale != 1.0:
            z = z * out_scale
        o_ref[...] = z.astype(o_ref.dtype)


def _projection(x, w, col_starts, width, out_dtypes, *, ts, tiles_per_seq, mod=None, bias=None,
                act=False, out_scale=1.0, name="projection"):
    n, d = x.shape
    tn = next(t for t in (1024, 512, width) if width % t == 0)
    n_out = len(col_starts)
    modulate = mod is not None
    in_specs = [pl.BlockSpec((ts, d), lambda i, j: (i, 0))]
    args = [x]
    if modulate:
        sc, sh, per_token = mod
        in_specs += [_mod_spec(per_token, ts, tiles_per_seq, d)] * 2
        args += [sc, sh]
    for c0 in col_starts:
        in_specs.append(pl.BlockSpec((d, tn), functools.partial(lambda i, j, o: (0, j + o), o=c0 // tn)))
        args.append(w)
    if bias is not None:
        for c0 in col_starts:
            in_specs.append(pl.BlockSpec((1, tn), functools.partial(lambda i, j, o: (0, j + o), o=c0 // tn)))
            args.append(bias)
    return pl.pallas_call(
        functools.partial(_proj_kernel, n_out=n_out, modulate=modulate, has_bias=bias is not None, act=act,
                          out_scale=out_scale),
        grid=(n // ts, width // tn),
        in_specs=in_specs,
        out_specs=[pl.BlockSpec((ts, tn), lambda i, j: (i, j))] * n_out,
        out_shape=[jax.ShapeDtypeStruct((n, width), dt) for dt in out_dtypes],
        scratch_shapes=[pltpu.VMEM((ts, d), BF16)],
        compiler_params=_cparams(("parallel", "arbitrary")),
        name=name,
    )(*args)


def _gate_kernel(v_ref, u_ref, g_ref, b_ref, ws_ref, bs_ref, o_ref, *maybe_vn_ref, n_groups):
    v = v_ref[0]
    vn = _layer_norm(v, g_ref[...], b_ref[...])
    if maybe_vn_ref:
        maybe_vn_ref[0][0] = vn
    vb = vn.astype(BF16)
    rows, d_a = v.shape
    gd = d_a // n_groups
    causal = (lax.broadcasted_iota(jnp.int32, (rows, rows), 0)
              >= lax.broadcasted_iota(jnp.int32, (rows, rows), 1))
    for g in range(n_groups):
        w = jnp.where(causal, ws_ref[g], 0.0).astype(BF16)
        s = jnp.dot(w, vb[:, g * gd:(g + 1) * gd], preferred_element_type=F32) + bs_ref[:, g:g + 1]
        o_ref[0, :, g * gd:(g + 1) * gd] = (u_ref[0, :, g * gd:(g + 1) * gd].astype(F32) * s).astype(o_ref.dtype)


def _spatial_gate(v_raw, u, vn_g, vn_b, w_s, b_s_t, *, emit_vn):
    c, r, d_a = v_raw.shape
    n_groups = w_s.shape[0]
    blk = pl.BlockSpec((1, r, d_a), lambda i: (i, 0, 0))
    out_shape = [jax.ShapeDtypeStruct((c, r, d_a), BF16)]
    out_specs = [blk]
    if emit_vn:
        out_shape.append(jax.ShapeDtypeStruct((c, r, d_a), F32))
        out_specs.append(blk)
    return pl.pallas_call(
        functools.partial(_gate_kernel, n_groups=n_groups),
        grid=(c,),
        in_specs=[blk, blk,
                  pl.BlockSpec((1, d_a), lambda i: (0, 0)),
                  pl.BlockSpec((1, d_a), lambda i: (0, 0)),
                  pl.BlockSpec((n_groups, r, r), lambda i: (0, 0, 0)),
                  pl.BlockSpec((r, n_groups), lambda i: (0, 0))],
        out_specs=out_specs,
        out_shape=out_shape,
        compiler_params=_cparams(("parallel",)),
        name="spatial_gate",
    )(v_raw, u, vn_g, vn_b, w_s, b_s_t)


def _route(logits, n_groups, n_per):
    lane = lax.broadcasted_iota(jnp.int32, logits.shape, 1)
    lane_f = lane.astype(F32)
    neg = -jnp.inf
    big = float(2 * ROUTE_WIDTH)
    gl = jnp.where(lane < n_groups, logits, neg)
    gmax = jnp.max(gl, axis=-1, keepdims=True)
    g_sel = jnp.min(jnp.where(gl == gmax, lane_f, big), axis=-1, keepdims=True)
    g_weight = 1.0 / jnp.sum(jnp.exp(gl - gmax), axis=-1, keepdims=True)
    e_lane = lane_f - float(n_groups)
    in_group = (e_lane >= g_sel * n_per) & (e_lane < (g_sel + 1.0) * n_per)
    el = jnp.where(in_group, logits, neg)
    v1 = jnp.max(el, axis=-1, keepdims=True)
    i1 = jnp.min(jnp.where(el == v1, lane_f, big), axis=-1, keepdims=True)
    el2 = jnp.where(lane_f == i1, neg, el)
    v2 = jnp.max(el2, axis=-1, keepdims=True)
    i2 = jnp.min(jnp.where(el2 == v2, lane_f, big), axis=-1, keepdims=True)
    e2 = jnp.exp(v2 - v1)
    gate1 = g_weight * (1.0 / (1.0 + e2))
    gate2 = g_weight * (e2 / (1.0 + e2))
    id1 = i1 - float(n_groups)
    id2 = i2 - float(n_groups)
    return jnp.where(lane == 0, id1,
                     jnp.where(lane == 1, id2,
                               jnp.where(lane == 2, gate1,
                                         jnp.where(lane == 3, gate2, 0.0))))


def _outproj_kernel(a_ref, w_ref, x_ref, g1_ref, lng_ref, lnb_ref, sc2_ref, sh2_ref, wr_ref, br_ref,
                    x1_ref, h2_ref, rt_ref, acc_ref, *, alpha, n_groups, n_per):
    k = pl.program_id(1)
    nk = pl.num_programs(1)
    ts = acc_ref.shape[0]

    def epilogue(r):
        g1 = g1_ref[0] if g1_ref.shape[1] == 1 else g1_ref[0, r, :]
        sc2 = sc2_ref[0] if sc2_ref.shape[1] == 1 else sc2_ref[0, r, :]
        sh2 = sh2_ref[0] if sh2_ref.shape[1] == 1 else sh2_ref[0, r, :]
        t = alpha * x_ref[r, :] + (1.0 + g1) * acc_ref[r, :]
        x1 = _layer_norm(t, lng_ref[...], lnb_ref[...])
        x1_ref[r, :] = x1
        h2 = x1 * (1.0 + sc2) + sh2
        h2_ref[r, :] = h2
        logits = jnp.dot(h2.astype(BF16), wr_ref[...], preferred_element_type=F32) + br_ref[...]
        rt_ref[r, :] = _route(logits, n_groups, n_per)

    @pl.when(k == 0)
    def _():
        acc_ref[...] = jnp.zeros_like(acc_ref)

    acc_ref[...] += jnp.dot(a_ref[...], w_ref[...], preferred_element_type=F32)

    @pl.when(k == nk - 1)
    def _():
        rows = min(EPILOGUE_ROWS, ts)

        def slab(s, c):
            epilogue(pl.ds(pl.multiple_of(s * rows, rows), rows))
            return c

        lax.fori_loop(0, ts // rows, slab, 0)


def _out_projection(a, w, x, g1, ln_g, ln_b, sc2, sh2, w_route, b_route, *, per_token, ts, tiles_per_seq,
                    alpha, n_groups, n_per):
    n, kdim = a.shape
    d = w.shape[1]
    tk = next(t for t in (1024, 512, kdim) if kdim % t == 0)
    mspec = _mod_spec(per_token, ts, tiles_per_seq, d)
    vec = pl.BlockSpec((1, d), lambda i, k: (0, 0))
    tile = pl.BlockSpec((ts, d), lambda i, k: (i, 0))
    return pl.pallas_call(
        functools.partial(_outproj_kernel, alpha=alpha, n_groups=n_groups, n_per=n_per),
        grid=(n // ts, kdim // tk),
        in_specs=[pl.BlockSpec((ts, tk), lambda i, k: (i, k)),
                  pl.BlockSpec((tk, d), lambda i, k: (k, 0)),
                  tile, mspec, vec, vec, mspec, mspec,
                  pl.BlockSpec((d, ROUTE_WIDTH), lambda i, k: (0, 0)),
                  pl.BlockSpec((1, ROUTE_WIDTH), lambda i, k: (0, 0))],
        out_specs=[tile, tile, pl.BlockSpec((ts, ROUTE_WIDTH), lambda i, k: (i, 0))],
        out_shape=[jax.ShapeDtypeStruct((n, d), F32), jax.ShapeDtypeStruct((n, d), F32),
                   jax.ShapeDtypeStruct((n, ROUTE_WIDTH), F32)],
        scratch_shapes=[pltpu.VMEM((ts, d), F32)],
        compiler_params=_cparams(("parallel", "arbitrary")),
        name="mixer_out_ln_route",
    )(a, w, x, g1, ln_g, ln_b, sc2, sh2, w_route, b_route)


def _moe_kernel(be_ref, nv_ref, na_ref, base_ref, ord_ref, ha_hbm, hb_hbm, wg_ref, wu_ref, wd_ref, out_hbm,
                xbuf, obuf, wgb, wub, wdb, gsem, ssem, *, tm, slot_stride, n_rows):
    i = pl.program_id(0)
    nb = pl.num_programs(0)
    slot = i % 2
    n_a = ha_hbm.shape[0]

    def gather_copy(blk, s, r, from_b):
        tok = ord_ref[base_ref[blk] + r] >> 1
        src = hb_hbm.at[pl.ds(tok - n_a, 1)] if from_b else ha_hbm.at[pl.ds(tok, 1)]
        return pltpu.make_async_copy(src, xbuf.at[s, pl.ds(r, 1)], gsem.at[s])

    def scatter_copy(blk, s, r):
        a = ord_ref[base_ref[blk] + r]
        dst = (a & 1) * slot_stride + (a >> 1)
        return pltpu.make_async_copy(obuf.at[s, pl.ds(r, 1)], out_hbm.at[pl.ds(dst, 1)], ssem.at[s])

    def for_rows(n, fn):
        n_groups = n // DMA_UNROLL

        def group(gidx, c):
            for u in range(DMA_UNROLL):
                fn(gidx * DMA_UNROLL + u, u)
            return c

        def tail(r, c):
            fn(r, 0)
            return c

        lax.fori_loop(0, n_groups, group, 0)
        lax.fori_loop(n_groups * DMA_UNROLL, n, tail, 0)

    def start_gather(blk, s):
        n_from_a = na_ref[blk]
        for_rows(n_from_a, lambda r, lane: gather_copy(blk, s, r, False).start())

        def from_b(r, c):
            gather_copy(blk, s, r, True).start()
            return c

        lax.fori_loop(n_from_a, nv_ref[blk], from_b, 0)

    def wait_rows(n, s, gather):
        def copy(rows):
            if gather:
                return pltpu.make_async_copy(ha_hbm.at[rows], xbuf.at[s, rows], gsem.at[s])
            return pltpu.make_async_copy(obuf.at[s, rows], out_hbm.at[rows], ssem.at[s])

        n_aligned = pl.multiple_of((n // SUBLANES) * SUBLANES, SUBLANES)

        @pl.when(n_aligned > 0)
        def _():
            copy(pl.ds(0, n_aligned)).wait()

        def tail(r, c):
            copy(pl.ds(0, 1)).wait()
            return c

        lax.fori_loop(n_aligned, n, tail, 0)

    prev = jnp.maximum(i - 1, 0)

    @pl.when(i == 0)
    def _():
        obuf[...] = jnp.zeros_like(obuf)
        for half in range(TOP_K):
            for start in range(n_rows, slot_stride, tm):
                size = min(tm, slot_stride - start)
                pltpu.sync_copy(obuf.at[0, pl.ds(0, size)], out_hbm.at[pl.ds(half * slot_stride + start, size)])
        xbuf[...] = jnp.zeros_like(xbuf)
        start_gather(0, 0)

    @pl.when(i + 1 < nb)
    def _():
        start_gather(jnp.minimum(i + 1, nb - 1), 1 - slot)

    @pl.when(i >= 2)
    def _():
        wait_rows(nv_ref[jnp.maximum(i - 2, 0)], slot, False)

    @pl.when(nv_ref[i] > 0)
    def _():
        wait_rows(nv_ref[i], slot, True)

        @pl.when((i == 0) | (be_ref[i] != be_ref[prev]))
        def _():
            wgb[...] = wg_ref[0, 0].astype(BF16)
            wub[...] = wu_ref[0, 0].astype(BF16)
            wdb[...] = wd_ref[0, 0].astype(BF16)

        x = xbuf[slot].astype(BF16)
        g = jnp.dot(x, wgb[...], preferred_element_type=F32)
        u = jnp.dot(x, wub[...], preferred_element_type=F32)
        mid = (g * jax.nn.sigmoid(g) * u).astype(BF16)
        o = jnp.dot(mid, wdb[...], preferred_element_type=F32)
        obuf[slot] = o
        for_rows(nv_ref[i], lambda r, lane: scatter_copy(i, slot, r).start(priority=lane % 2))

    @pl.when(i == nb - 1)
    def _():
        @pl.when(i >= 1)
        def _():
            wait_rows(nv_ref[prev], 1 - slot, False)
        wait_rows(nv_ref[i], slot, False)


def _moe_experts(h2_a, h2_b, blk_expert, blk_nvalid, blk_from_a, blk_base, order, w_gate, w_up, w_down, layer,
                 *, tm, slot_stride):
    d = h2_a.shape[1]
    n = h2_a.shape[0] + h2_b.shape[0]
    _, n_experts, _, de = w_gate.shape
    nb = blk_expert.shape[0]

    def expert_block(i, be, nv, na, base, order):
        return (layer, be[i], 0, 0)

    grid_spec = pltpu.PrefetchScalarGridSpec(
        num_scalar_prefetch=5,
        grid=(nb,),
        in_specs=[pl.BlockSpec(memory_space=pl.ANY), pl.BlockSpec(memory_space=pl.ANY),
                  pl.BlockSpec((1, 1, d, de), expert_block),
                  pl.BlockSpec((1, 1, d, de), expert_block),
                  pl.BlockSpec((1, 1, de, d), expert_block)],
        out_specs=pl.BlockSpec(memory_space=pl.ANY),
        scratch_shapes=[pltpu.VMEM((2, tm, d), F32), pltpu.VMEM((2, tm, d), F32),
                        pltpu.VMEM((d, de), BF16), pltpu.VMEM((d, de), BF16), pltpu.VMEM((de, d), BF16),
                        pltpu.SemaphoreType.DMA((2,)), pltpu.SemaphoreType.DMA((2,))],
    )
    return pl.pallas_call(
        functools.partial(_moe_kernel, tm=tm, slot_stride=slot_stride, n_rows=n),
        grid_spec=grid_spec,
        out_shape=jax.ShapeDtypeStruct((TOP_K * slot_stride, d), F32),
        compiler_params=_cparams(("arbitrary",)),
        name="moe_experts",
    )(blk_expert, blk_nvalid, blk_from_a, blk_base, order, h2_a, h2_b, w_gate, w_up, w_down)


def _dispatch(route, n_experts, tm, n_first):
    n = route.shape[0]
    eid = route[:, :TOP_K].astype(jnp.int32).reshape(-1)
    n_assign = n * TOP_K
    nb = -(-n_assign // tm) + n_experts
    _, order = lax.sort((eid, jnp.arange(n_assign, dtype=jnp.int32)), num_keys=1)
    experts = jnp.arange(n_experts, dtype=jnp.int32)
    starts = jnp.sum(eid[None, :] < experts[:, None], axis=1, dtype=jnp.int32)
    counts = jnp.sum(eid[None, :] == experts[:, None], axis=1, dtype=jnp.int32)
    counts_first = jnp.sum(eid[None, :n_first * TOP_K] == experts[:, None], axis=1, dtype=jnp.int32)
    nblk = (counts + tm - 1) // tm
    blk_end = jnp.cumsum(nblk)
    blk_start = blk_end - nblk
    blk = jnp.arange(nb, dtype=jnp.int32)
    n_used = blk_end[-1]
    be = jnp.minimum(jnp.sum(blk_end[None, :] <= blk[:, None], axis=1, dtype=jnp.int32), n_experts - 1)
    mine = be[:, None] == experts[None, :]

    def of_expert(per_expert):
        return jnp.sum(jnp.where(mine, per_expert[None, :], 0), axis=1, dtype=jnp.int32)

    used = blk < n_used
    first_row = (blk - of_expert(blk_start)) * tm
    nv = jnp.where(used, jnp.clip(of_expert(counts) - first_row, 0, tm), 0).astype(jnp.int32)
    n_from_first = jnp.clip(of_expert(counts_first) - first_row, 0, nv).astype(jnp.int32)
    base = jnp.where(used, of_expert(starts) + first_row, 0).astype(jnp.int32)
    last_e = jnp.max(jnp.where(used, be, 0))
    be = jnp.where(used, be, last_e)
    return be, nv, n_from_first, base, order


def _combine_kernel(x1_ref, oa_ref, ob_ref, rt_ref, g2_ref, lng_ref, lnb_ref, x2_ref, *, alpha):
    y = oa_ref[...] * rt_ref[:, TOP_K:TOP_K + 1] + ob_ref[...] * rt_ref[:, TOP_K + 1:TOP_K + 2]
    t = alpha * x1_ref[...] + (1.0 + g2_ref[0]) * y
    x2_ref[...] = _layer_norm(t, lng_ref[...], lnb_ref[...])


def _combine(x1, out2, route, first_row, slot_stride, g2, ln_g, ln_b, *, per_token, ts, tiles_per_seq, alpha):
    n, d = x1.shape
    vec = pl.BlockSpec((1, d), lambda i: (0, 0))
    tile = pl.BlockSpec((ts, d), lambda i: (i, 0))
    off_a = first_row // ts
    off_b = (first_row + slot_stride) // ts
    return pl.pallas_call(
        functools.partial(_combine_kernel, alpha=alpha),
        grid=(n // ts,),
        in_specs=[tile,
                  pl.BlockSpec((ts, d), lambda i: (i + off_a, 0)),
                  pl.BlockSpec((ts, d), lambda i: (i + off_b, 0)),
                  pl.BlockSpec((ts, ROUTE_WIDTH), lambda i: (i, 0)),
                  _mod_spec(per_token, ts, tiles_per_seq, d), vec, vec],
        out_specs=tile,
        out_shape=jax.ShapeDtypeStruct((n, d), F32),
        compiler_params=_cparams(("parallel",)),
        name="moe_combine_ln",
    )(x1, out2, out2, route, g2, ln_g, ln_b)


def _split2(x):
    hi = x.astype(BF16)
    return hi, (x - hi.astype(F32)).astype(BF16)


def _stick_terms(z, vis, excl_ones):
    lb = _log_sigmoid(z)
    ls = lb - z
    if vis is not None:
        ls = jnp.where(vis, ls, 0.0)
    hi, lo = _split2(ls)
    later = jnp.dot(hi, excl_ones, preferred_element_type=F32) + jnp.dot(lo, excl_ones, preferred_element_type=F32)
    return lb, later, jnp.sum(ls, axis=-1, keepdims=True)


def _stick_weights(lb, later, run, vis):
    a = jnp.exp(lb + (later + run))
    if vis is not None:
        a = jnp.where(vis, a, 0.0)
    return a


def _stick_block(z, vis, excl_ones, run):
    lb, later, total = _stick_terms(z, vis, excl_ones)
    return _stick_weights(lb, later, run, vis), run + total


def _excl_ones(tk):
    j = lax.broadcasted_iota(jnp.int32, (tk, tk), 0)
    s = lax.broadcasted_iota(jnp.int32, (tk, tk), 1)
    return jnp.where(j > s, 1.0, 0.0).astype(BF16)


def _attn_kernel(bias_ref, q_ref, k_ref, v_ref, o_ref, *, t, dh, heads):
    scale = dh ** -0.5
    hg = pl.program_id(1)
    qi = pl.program_id(2)
    ones = _excl_ones(t)
    below = (lax.broadcasted_iota(jnp.int32, (t, t), 1) < lax.broadcasted_iota(jnp.int32, (t, t), 0))

    def blocks(starts, carry, masks):
        out = []
        for hh in range(heads):
            acc, run = carry[hh]
            lanes = slice(hh * dh, (hh + 1) * dh)
            terms = []
            for start, vis in zip(starts, masks):
                k = k_ref[0, pl.ds(start, t), lanes].astype(BF16)
                z = lax.dot_general(q_ref[0, :, lanes], k, (((1,), (1,)), ((), ())),
                                    preferred_element_type=F32) * scale + bias_ref[hg * heads + hh]
                terms.append(_stick_terms(z, vis, ones))
            for start, vis, (lb, later, total) in zip(starts, masks, terms):
                a = _stick_weights(lb, later, run, vis)
                run = run + total
                v = v_ref[0, pl.ds(start, t), lanes].astype(BF16)
                acc = acc + jnp.dot(a.astype(BF16), v, preferred_element_type=F32)
            out.append((acc, run))
        return tuple(out)

    init = tuple((jnp.zeros((t, dh), F32), jnp.zeros((t, 1), F32)) for _ in range(heads))
    diag = pl.multiple_of(qi * t, t)
    carry = lax.cond(qi == 0,
                     lambda: blocks([diag], init, [below]),
                     lambda: blocks([diag, pl.multiple_of(diag - t, t)], init, [below, None]))
    rest = jnp.maximum(qi - 1, 0)

    def pair(j, c):
        newer = pl.multiple_of((rest - 1 - 2 * j) * t, t)
        return blocks([newer, pl.multiple_of(newer - t, t)], c, [None, None])

    carry = lax.fori_loop(0, rest // 2, pair, carry)
    carry = lax.fori_loop(0, rest % 2, lambda j, c: blocks([0], c, [None]), carry)
    for hh in range(heads):
        o_ref[0, :, hh * dh:(hh + 1) * dh] = carry[hh][0].astype(o_ref.dtype)


def _attention_prompt(q, k, v, logit_bias, *, n_heads):
    b, s, hd = q.shape
    dh = hd // n_heads
    t = min(256, s)
    heads = next(h for h in (4, 2, 1) if n_heads % h == 0)
    w = heads * dh
    return pl.pallas_call(
        functools.partial(_attn_kernel, t=t, dh=dh, heads=heads),
        grid=(b, n_heads // heads, s // t),
        in_specs=[pl.BlockSpec(memory_space=pltpu.SMEM),
                  pl.BlockSpec((1, t, w), lambda bi, h, qi: (bi, qi, h)),
                  pl.BlockSpec((1, s, w), lambda bi, h, qi: (bi, 0, h)),
                  pl.BlockSpec((1, s, w), lambda bi, h, qi: (bi, 0, h))],
        out_specs=pl.BlockSpec((1, t, w), lambda bi, h, qi: (bi, qi, h)),
        out_shape=jax.ShapeDtypeStruct((b, s, hd), BF16),
        compiler_params=_cparams(("parallel", "parallel", "arbitrary")),
        name="stick_attention_prompt",
    )(logit_bias, q, k, v)


def _attn_paged_kernel(pt_ref, q_ref, bias_ref, own_ref, pick_ref, spread_ref, *rest, page, nq, pps):
    k_refs, v_refs = rest[:pps], rest[pps:2 * pps]
    kn_ref, vn_ref, o_ref, acc_ref, run_ref = rest[2 * pps:]
    step = pl.program_id(1)
    scale = q_ref.shape[-1] ** -0.5

    @pl.when(step == 0)
    def _():
        acc_ref[...] = jnp.zeros_like(acc_ref)
        run_ref[...] = jnp.zeros_like(run_ref)

    def process(pages, new_keys):
        own = own_ref[...]
        ones = _excl_ones(page)
        terms = []
        for k_ref, _ in pages:
            s_all = lax.dot_general(q_ref[0], k_ref[0].astype(BF16), (((1,), (1,)), ((), ())),
                                    preferred_element_type=F32)
            hi, lo = _split2(s_all * own)
            z = (jnp.dot(hi, pick_ref[...], preferred_element_type=F32)
                 + jnp.dot(lo, pick_ref[...], preferred_element_type=F32)) * scale + bias_ref[...]
            vis = None
            if new_keys:
                t = lax.broadcasted_iota(jnp.int32, z.shape, 0) % nq
                vis = lax.broadcasted_iota(jnp.int32, z.shape, 1) < t
            terms.append(_stick_terms(z, vis, ones) + (vis,))
        run = run_ref[...]
        acc = acc_ref[...]
        for (lb, later, total, vis), (_, v_ref) in zip(terms, pages):
            a = _stick_weights(lb, later, run, vis)
            run = run + total
            a_all = (jnp.dot(a.astype(BF16), spread_ref[...], preferred_element_type=F32) * own).astype(BF16)
            acc = acc + jnp.dot(a_all, v_ref[0].astype(BF16), preferred_element_type=F32)
        run_ref[...] = run
        acc_ref[...] = acc

    @pl.when(step == 0)
    def _():
        process([(kn_ref, vn_ref)], True)

    @pl.when(step > 0)
    def _():
        process(list(zip(k_refs, v_refs)), False)

    @pl.when(step == pl.num_programs(1) - 1)
    def _():
        o_ref[0] = acc_ref[...]


def _attention_paged(q_rows, bias_col, cache_k, cache_v, k_new, v_new, page_table, *, n_heads, nq):
    b, rows, dh = q_rows.shape
    n_pages = page_table.shape[1]
    cols = cache_k.shape[1]
    page = cols // n_heads
    col = jnp.arange(cols, dtype=jnp.int32)
    own = (col[None, :] % n_heads == jnp.arange(rows, dtype=jnp.int32)[:, None] // nq).astype(F32)
    pick = (col[:, None] // n_heads == jnp.arange(page, dtype=jnp.int32)[None, :]).astype(BF16)

    pps = next(p for p in (PAGES_PER_STEP, 2, 1) if n_pages % p == 0)

    def page_spec(j):
        return pl.BlockSpec(
            (1, cols, dh), lambda bi, s, pt: (pt[bi, n_pages - 1 - (jnp.maximum(s, 1) - 1) * pps - j], 0, 0))

    def const(shape):
        return pl.BlockSpec(shape, lambda bi, s, pt: (0,) * len(shape))

    grid_spec = pltpu.PrefetchScalarGridSpec(
        num_scalar_prefetch=1,
        grid=(b, n_pages // pps + 1),
        in_specs=[pl.BlockSpec((1, rows, dh), lambda bi, s, pt: (bi, 0, 0)),
                  const((rows, 1)), const((rows, cols)), const((cols, page)), const((page, cols))]
                 + [page_spec(j) for j in range(pps)] * 2
                 + [pl.BlockSpec((1, cols, dh), lambda bi, s, pt: (bi, 0, 0))] * 2,
        out_specs=pl.BlockSpec((1, rows, dh), lambda bi, s, pt: (bi, 0, 0)),
        scratch_shapes=[pltpu.VMEM((rows, dh), F32), pltpu.VMEM((rows, 1), F32)],
    )
    return pl.pallas_call(
        functools.partial(_attn_paged_kernel, page=page, nq=nq, pps=pps),
        grid_spec=grid_spec,
        out_shape=jax.ShapeDtypeStruct((b, rows, dh), F32),
        compiler_params=_cparams(("parallel", "arbitrary")),
        name="stick_attention_paged",
    )(page_table, q_rows, bias_col, own, pick, jnp.transpose(pick), *([cache_k] * pps), *([cache_v] * pps),
      k_new, v_new)


def kernel(x_prompt, x_sample, c_prompt, c_sample, cache_k, cache_v, page_table, ada_w, ada_b, ln1_g, ln1_b, ln2_g, ln2_b, a_w_in, a_b_in, a_vnorm_g, a_vnorm_b, a_w_s, a_b_s, a_w_out, kv_w, b_w_q, b_w_o, b_logit_bias, moe_router_group_w, moe_router_group_b, moe_router_expert_w, moe_router_expert_b, moe_w_gate, moe_w_up, moe_w_down):
    bp, sp, d = x_prompt.shape
    bs, ss, _ = x_sample.shape
    depth = ada_w.shape[0]
    n_a = a_w_in.shape[0]
    d_a = a_w_out.shape[1]
    chunk = a_w_s.shape[-1]
    n_pool, page, n_heads, dh = cache_k.shape
    hd = n_heads * dh
    n_groups, n_per = moe_router_expert_b.shape[1:]
    n_experts = n_groups * n_per
    alpha = (2 * depth) ** 0.25
    assert n_groups + n_experts <= ROUTE_WIDTH and ss <= SUBLANES and sp % chunk == 0
    np_tok, ns_tok = bp * sp, bs * ss
    ts_p = min(512, sp)
    tiles_p = sp // ts_p
    assert np_tok % ns_tok == 0

    c_rows = -(-(bp + bs) // SUBLANES) * SUBLANES
    c_all = jnp.concatenate([c_prompt, c_sample, jnp.zeros((c_rows - bp - bs, d), F32)], axis=0)
    mod = _modulation(c_all, ada_w, ada_b)

    def mods(l, group):
        m = mod[l].reshape(c_rows, 6, d)
        if group == 0:
            return [m[:bp, j].reshape(bp, 1, d) for j in range(6)]
        return [jnp.repeat(m[bp:bp + bs, j], ss, axis=0).reshape(1, ns_tok, d) for j in range(6)]

    groups = [dict(x=x_prompt.reshape(np_tok, d), ts=ts_p, tiles=tiles_p, per_token=False, n=np_tok),
              dict(x=x_sample.reshape(ns_tok, d), ts=ns_tok, tiles=1, per_token=True, n=ns_tok)]
    tm = 128
    slot_stride = -(-(np_tok + ns_tok) // ts_p) * ts_p
    assert slot_stride % ns_tok == 0 and (n_heads * ss) % SUBLANES == 0
    chunk_v = []
    k_out, v_out = [None, None], [None, None]

    for l in range(depth):
        w_route = jnp.concatenate(
            [moe_router_group_w[l],
             jnp.transpose(moe_router_expert_w[l], (1, 0, 2)).reshape(d, n_experts),
             jnp.zeros((d, ROUTE_WIDTH - n_groups - n_experts), F32)], axis=1).astype(BF16)
        b_route = jnp.concatenate(
            [moe_router_group_b[l], moe_router_expert_b[l].reshape(-1),
             jnp.zeros((ROUTE_WIDTH - n_groups - n_experts,), F32)]).reshape(1, ROUTE_WIDTH)
        x1s, h2s, routes, g2s = [], [], [], []
        for gi, grp in enumerate(groups):
            sh1, sc1, g1, sh2, sc2, g2 = mods(l, gi)
            x, ts, tiles, per_token = grp["x"], grp["ts"], grp["tiles"], grp["per_token"]
            if l < n_a:
                u, v_raw = _projection(x, a_w_in[l].astype(BF16), (0, d_a), d_a, (F32, F32), ts=ts,
                                       tiles_per_seq=tiles, mod=(sc1, sh1, per_token),
                                       bias=a_b_in[l].reshape(1, 2 * d_a), act=True, name="gmlp_in")
                bs_t = jnp.transpose(a_b_s[l])
                if gi == 0:
                    n_chunks = np_tok // chunk
                    gated, = _spatial_gate(v_raw.reshape(n_chunks, chunk, d_a), u.reshape(n_chunks, chunk, d_a),
                                           a_vnorm_g[l].reshape(1, d_a), a_vnorm_b[l].reshape(1, d_a),
                                           a_w_s[l], bs_t, emit_vn=False)
                    a_mix = gated.reshape(np_tok, d_a)
                else:
                    pad = ((0, 0), (0, SAMPLE_ROWS - ss), (0, 0))
                    gated, vn = _spatial_gate(jnp.pad(v_raw.reshape(bs, ss, d_a), pad),
                                              jnp.pad(u.reshape(bs, ss, d_a), pad),
                                              a_vnorm_g[l].reshape(1, d_a), a_vnorm_b[l].reshape(1, d_a),
                                              a_w_s[l][:, :SAMPLE_ROWS, :SAMPLE_ROWS], bs_t[:SAMPLE_ROWS], emit_vn=True)
                    a_mix = gated[:, :ss].reshape(ns_tok, d_a)
                    chunk_v.append(vn[:, :ss])
                w_mix = a_w_out[l].astype(BF16)
            else:
                j = l - n_a
                if k_out[gi] is None:
                    k_out[gi], v_out[gi] = _projection(x, kv_w.astype(BF16), (0, hd), hd, (F32, F32), ts=ts,
                                                       tiles_per_seq=tiles, name="kv_proj")
                q, = _projection(x, b_w_q[j].astype(BF16), (0,), hd, (BF16,), ts=ts, tiles_per_seq=tiles,
                                 mod=(sc1, sh1, per_token), name="q_proj")
                if gi == 0:
                    o = _attention_prompt(q.reshape(bp, sp, hd), k_out[gi].reshape(bp, sp, hd),
                                          v_out[gi].reshape(bp, sp, hd), b_logit_bias[j], n_heads=n_heads)
                    a_mix = o.reshape(np_tok, hd)
                else:
                    q_rows = jnp.transpose(q.reshape(bs, ss, n_heads, dh), (0, 2, 1, 3)).reshape(bs, n_heads * ss, dh)

                    def new_page(t):
                        t = jnp.pad(t.reshape(bs, ss, n_heads, dh), ((0, 0), (0, page - ss), (0, 0), (0, 0)))
                        return t.reshape(bs, page * n_heads, dh)

                    o = _attention_paged(q_rows, jnp.repeat(b_logit_bias[j], ss).reshape(n_heads * ss, 1),
                                         cache_k.reshape(n_pool, page * n_heads, dh),
                                         cache_v.reshape(n_pool, page * n_heads, dh),
                                         new_page(k_out[gi]), new_page(v_out[gi]), page_table,
                                         n_heads=n_heads, nq=ss)
                    o = jnp.transpose(o.reshape(bs, n_heads, ss, dh), (0, 2, 1, 3))
                    a_mix = o.reshape(ns_tok, hd).astype(BF16)
                w_mix = b_w_o[j].astype(BF16)
            x1, h2, route = _out_projection(a_mix, w_mix, x, g1, ln1_g[l].reshape(1, d), ln1_b[l].reshape(1, d),
                                            sc2, sh2, w_route, b_route, per_token=per_token, ts=ts,
                                            tiles_per_seq=tiles, alpha=alpha, n_groups=n_groups, n_per=n_per)
            x1s.append(x1)
            h2s.append(h2)
            routes.append(route)
            g2s.append(g2)

        be, nv, n_prompt_rows, base, order = _dispatch(jnp.concatenate(routes, axis=0), n_experts, tm, np_tok)
        out2 = _moe_experts(h2s[0], h2s[1], be, nv, n_prompt_rows, base, order, moe_w_gate, moe_w_up,
                            moe_w_down, l, tm=tm, slot_stride=slot_stride)
        first_rows = [0, np_tok]
        for gi, grp in enumerate(groups):
            grp["x"] = _combine(x1s[gi], out2, routes[gi], first_rows[gi], slot_stride, g2s[gi],
                                ln2_g[l].reshape(1, d), ln2_b[l].reshape(1, d), per_token=grp["per_token"],
                                ts=grp["ts"], tiles_per_seq=grp["tiles"], alpha=alpha)

    y_prompt = groups[0]["x"].reshape(bp, sp, d)
    y_sample = groups[1]["x"].reshape(bs, ss, d)
    return (y_prompt, y_sample,
            k_out[0].reshape(bp, sp, n_heads, dh), v_out[0].reshape(bp, sp, n_heads, dh),
            k_out[1].reshape(bs, ss, n_heads, dh), v_out[1].reshape(bs, ss, n_heads, dh),
            jnp.stack(chunk_v))
```
